```python
import jax
import jax.numpy as jnp
from jax import lax
import numpy as np

D_MODEL = 4096
BATCH = 2
SEQ = 8192
DEPTH = 4

CTX_LEN = 256
GRID_W = 64
N_MOD = 6
ADA_RANK = 256
HEAD_DIM = 128
MLA_HEADS = 3 * D_MODEL // (8 * HEAD_DIM)
NA_HEADS = 3 * D_MODEL // (8 * HEAD_DIM)
CONV_CH = D_MODEL - (MLA_HEADS + NA_HEADS) * HEAD_DIM
MLA_Q_RANK = D_MODEL // 4
MLA_KV_RANK = D_MODEL // 8
MLA_NOPE = HEAD_DIM
MLA_ROPE = 64
MLA_V = HEAD_DIM
MLA_QK = MLA_NOPE + MLA_ROPE
ROPE_F = MLA_ROPE // 4
ROPE_BASE = 10000.0
CONV_W = 3
NA_DIM = HEAD_DIM
NA_WIN_R = 8
NA_WIN_C = 16
N_EXPERTS = 16
CAP_FACTOR = 2
EXPERT_FF = 768
Q_BLOCK = 128
EPS = 1e-6
MLA_SCALE = MLA_QK ** -0.5
NA_SCALE = NA_DIM ** -0.5

OFF_CQ = 0
OFF_CKV = OFF_CQ + MLA_Q_RANK
OFF_KR = OFF_CKV + MLA_KV_RANK
OFF_CB = OFF_KR + MLA_ROPE
OFF_CC = OFF_CB + CONV_CH
OFF_CX = OFF_CC + CONV_CH
OFF_NQ = OFF_CX + CONV_CH
OFF_NK = OFF_NQ + NA_HEADS * NA_DIM
OFF_NV = OFF_NK + NA_HEADS * NA_DIM
D_IN = OFF_NV + NA_HEADS * NA_DIM

kernel_name = "hybrid_mla_conv_natten_ecmoe_dit"


def rms_norm(x, g):
    xf = x.astype(jnp.float32)
    y = xf * lax.rsqrt(jnp.mean(xf * xf, axis=-1, keepdims=True) + EPS)
    return (y * g.astype(jnp.float32)).astype(x.dtype)


def ada_mod(cond, down, up, bias):
    m = (jax.nn.silu(cond) @ down) @ up + bias
    return jnp.split(m, N_MOD, axis=-1)


def modulate(h, g, shift, scale):
    return rms_norm(h, g) * (1 + scale) + shift


def axial_rope_tables(n):
    pos = jnp.arange(n, dtype=jnp.int32)
    row = (pos // GRID_W).astype(jnp.float32)
    col = (pos % GRID_W).astype(jnp.float32)
    inv = jnp.power(ROPE_BASE, -jnp.arange(ROPE_F, dtype=jnp.float32) / ROPE_F)
    ang = jnp.stack([row[:, None] * inv, col[:, None] * inv], axis=1)
    return jnp.cos(ang), jnp.sin(ang)


def apply_axial_rope(x, cos, sin):
    xs = x.reshape(x.shape[:-1] + (2, 2, ROPE_F)).astype(jnp.float32)
    x1, x2 = xs[..., 0, :], xs[..., 1, :]
    cs, sn = cos[:, None], sin[:, None]
    out = jnp.stack([x1 * cs - x2 * sn, x2 * cs + x1 * sn], axis=-2)
    return out.reshape(x.shape).astype(x.dtype)


def rope_tail(t, rope):
    if rope is None:
        return t
    cos, sin = rope
    return jnp.concatenate([t[..., :MLA_NOPE], apply_axial_rope(t[..., MLA_NOPE:], cos, sin)], axis=-1)


def mla_q(p_cq, g_qa, w_uq, g_q, rope):
    q = (rms_norm(p_cq, g_qa) @ w_uq).reshape(p_cq.shape[:-1] + (MLA_HEADS, MLA_QK))
    return rope_tail(rms_norm(q, g_q), rope)


def mla_kv(p, g_kva, w_ukv, g_k, rope):
    ckv = rms_norm(p[..., :MLA_KV_RANK], g_kva)
    kv = (ckv @ w_ukv).reshape(p.shape[:-1] + (MLA_HEADS, MLA_NOPE + MLA_V))
    k_nope, v = kv[..., :MLA_NOPE], kv[..., MLA_NOPE:]
    k_rope = jnp.broadcast_to(p[..., None, MLA_KV_RANK:], k_nope.shape[:-1] + (MLA_ROPE,))
    k = rms_norm(jnp.concatenate([k_nope, k_rope], axis=-1), g_k)
    return rope_tail(k, rope), v


def na_q(p, g_q):
    return rms_norm(p.reshape(p.shape[:-1] + (NA_HEADS, NA_DIM)), g_q)


def na_kv(p, g_k):
    hd = NA_HEADS * NA_DIM
    k = rms_norm(p[..., :hd].reshape(p.shape[:-1] + (NA_HEADS, NA_DIM)), g_k)
    v = p[..., hd:].reshape(p.shape[:-1] + (NA_HEADS, NA_DIM))
    return k, v


def dense_attention(q, k, v, scale):
    b, n, h, dq = q.shape
    nb = n // Q_BLOCK
    qb = q.reshape(b, nb, Q_BLOCK, h, dq).transpose(1, 0, 2, 3, 4)

    def one(qblk):
        s = jnp.einsum('bqhd,bkhd->bhqk', qblk, k, preferred_element_type=jnp.float32) * scale
        p = jax.nn.softmax(s, axis=-1).astype(v.dtype)
        return jnp.einsum('bhqk,bkhd->bqhd', p, v)

    o = lax.map(one, qb)
    return o.transpose(1, 0, 2, 3, 4).reshape(b, n, h, v.shape[-1])


def neighbourhood_attention(q, k, v, k_ctx, v_ctx, rpb):
    b, s, h, d = q.shape
    rows = s // GRID_W
    wr = min(NA_WIN_R, rows)
    wc = min(NA_WIN_C, GRID_W)
    n_ctx = k_ctx.shape[1]
    qg = q.reshape(b, rows, GRID_W, h, d).transpose(1, 0, 2, 3, 4)
    kg = k.reshape(b, rows, GRID_W, h, d)
    vg = v.reshape(b, rows, GRID_W, h, d)
    cols = jnp.arange(GRID_W)
    c0 = jnp.clip(cols - wc // 2, 0, GRID_W - wc)
    col_idx = c0[:, None] + jnp.arange(wc)[None, :]
    col_off = col_idx - cols[:, None] + (NA_WIN_C - 1)
    rpb32 = rpb.astype(jnp.float32)

    def one(args):
        r, qr = args
        r0 = jnp.clip(r - wr // 2, 0, rows - wr)
        kw = jnp.take(lax.dynamic_slice_in_dim(kg, r0, wr, axis=1), col_idx, axis=2)
        vw = jnp.take(lax.dynamic_slice_in_dim(vg, r0, wr, axis=1), col_idx, axis=2)
        row_off = r0 + jnp.arange(wr) - r + (NA_WIN_R - 1)
        bias = rpb32[:, row_off[None, :, None], col_off[:, None, :]]
        s_loc = jnp.einsum('bqhd,baqchd->bhqac', qr, kw, preferred_element_type=jnp.float32) * NA_SCALE + bias
        s_ctx = jnp.einsum('bqhd,bkhd->bhqk', qr, k_ctx, preferred_element_type=jnp.float32) * NA_SCALE
        p = jax.nn.softmax(jnp.concatenate([s_ctx, s_loc.reshape(b, h, GRID_W, wr * wc)], axis=-1), axis=-1)
        p = p.astype(v.dtype)
        p_ctx = p[..., :n_ctx]
        p_loc = p[..., n_ctx:].reshape(b, h, GRID_W, wr, wc)
        return (jnp.einsum('bhqk,bkhd->bqhd', p_ctx, v_ctx)
                + jnp.einsum('bhqac,baqchd->bqhd', p_loc, vw))

    o = lax.map(one, (jnp.arange(rows), qg))
    return o.transpose(1, 0, 2, 3, 4).reshape(b, s, h, d)


def short_gated_conv(p_b, p_c, p_x, w):
    n = p_x.shape[1]
    u = jnp.pad(p_c * p_x, ((0, 0), (CONV_W // 2, CONV_W // 2), (0, 0)))
    y = sum(w[i] * u[:, i:i + n] for i in range(CONV_W))
    return p_b * y


def expert_choice_moe(hn, w_router, w_gate, w_up, w_down):
    b, n, _ = hn.shape
    cap = CAP_FACTOR * n // N_EXPERTS
    aff = jax.nn.softmax(jnp.einsum('bnd,de->bne', hn, w_router, preferred_element_type=jnp.float32), axis=-1)
    g, idx = lax.top_k(aff.transpose(0, 2, 1), cap)
    xe = jax.vmap(lambda hb, ib: hb[ib])(hn, idx)
    a = jnp.einsum('becd,edf->becf', xe, w_gate)
    u = jnp.einsum('becd,edf->becf', xe, w_up)
    y = jnp.einsum('becf,efd->becd', jax.nn.silu(a) * u, w_down) * g[..., None].astype(hn.dtype)
    bidx = jnp.arange(b)[:, None, None]
    return jnp.zeros_like(hn).at[bidx, idx].add(y)


def setup_inputs(seed: int = 0) -> dict:
    key = jax.random.key(seed)
    ks = iter(jax.random.split(key, 32))
    L = DEPTH

    def nrm(shape, scale):
        return jax.random.normal(next(ks), shape, jnp.float32) * scale

    def gain(shape):
        return 1.0 + nrm(shape, 0.02)

    return {
        "x": nrm((BATCH, SEQ, D_MODEL), 1.0),
        "c": nrm((BATCH, D_MODEL), 1.0),
        "ctx": nrm((BATCH, CTX_LEN, D_MODEL), 1.0),
        "c_ctx": nrm((D_MODEL,), 1.0),
        "ada_down": nrm((L, D_MODEL, ADA_RANK), D_MODEL ** -0.5),
        "ada_up": nrm((L, ADA_RANK, N_MOD * D_MODEL), 0.3 * ADA_RANK ** -0.5),
        "ada_bias": nrm((L, N_MOD * D_MODEL), 0.01),
        "norm1_g": gain((L, D_MODEL)),
        "w_in": nrm((L, D_MODEL, D_IN), D_MODEL ** -0.5),
        "mla_qa_g": gain((L, MLA_Q_RANK)),
        "mla_kva_g": gain((L, MLA_KV_RANK)),
        "mla_w_uq": nrm((L, MLA_Q_RANK, MLA_HEADS * MLA_QK), MLA_Q_RANK ** -0.5),
        "mla_w_ukv": nrm((L, MLA_KV_RANK, MLA_HEADS * (MLA_NOPE + MLA_V)), MLA_KV_RANK ** -0.5),
        "mla_q_g": gain((L, MLA_QK)),
        "mla_k_g": gain((L, MLA_QK)),
        "conv_w": nrm((L, CONV_W, CONV_CH), CONV_W ** -0.5),
        "na_q_g": gain((L, NA_DIM)),
        "na_k_g": gain((L, NA_DIM)),
        "na_rpb": nrm((L, NA_HEADS, 2 * NA_WIN_R - 1, 2 * NA_WIN_C - 1), 0.2),
        "w_out": nrm((L, D_MODEL, D_MODEL), D_MODEL ** -0.5),
        "norm2_g": gain((L, D_MODEL)),
        "w_router": nrm((L, D_MODEL, N_EXPERTS), D_MODEL ** -0.5),
        "ex_gate": nrm((L, N_EXPERTS, D_MODEL, EXPERT_FF), D_MODEL ** -0.5),
        "ex_up": nrm((L, N_EXPERTS, D_MODEL, EXPERT_FF), D_MODEL ** -0.5),
        "ex_down": nrm((L, N_EXPERTS, EXPERT_FF, D_MODEL), EXPERT_FF ** -0.5),
    }


def reference(x, c, ctx, c_ctx, ada_down, ada_up, ada_bias, norm1_g, w_in, mla_qa_g, mla_kva_g,
              mla_w_uq, mla_w_ukv, mla_q_g, mla_k_g, conv_w, na_q_g, na_k_g, na_rpb, w_out,
              norm2_g, w_router, ex_gate, ex_up, ex_down):
    b, s, _ = x.shape
    rope = axial_rope_tables(s)
    h, hc = x, ctx
    for l in range(DEPTH):
        last = l == DEPTH - 1
        sx1, cx1, gx1, sx2, cx2, gx2 = ada_mod(c[:, None, :], ada_down[l], ada_up[l], ada_bias[l])
        sc1, cc1, gc1, sc2, cc2, gc2 = ada_mod(c_ctx, ada_down[l], ada_up[l], ada_bias[l])
        w = w_in[l]

        nc = modulate(hc, norm1_g[l], sc1, cc1)
        if last:
            pc_mla_kv = nc @ w[:, OFF_CKV:OFF_CB]
            pc_na_kv = nc @ w[:, OFF_NK:D_IN]
        else:
            pc = nc @ w
            pc_mla_kv = pc[..., OFF_CKV:OFF_CB]
            pc_na_kv = pc[..., OFF_NK:]
        kc_a, vc_a = mla_kv(pc_mla_kv, mla_kva_g[l], mla_w_ukv[l], mla_k_g[l], None)
        kc_c, vc_c = na_kv(pc_na_kv, na_k_g[l])

        nx = modulate(h, norm1_g[l], sx1, cx1)
        px = nx @ w
        q_a = mla_q(px[..., OFF_CQ:OFF_CKV], mla_qa_g[l], mla_w_uq[l], mla_q_g[l], rope)
        k_a, v_a = mla_kv(px[..., OFF_CKV:OFF_CB], mla_kva_g[l], mla_w_ukv[l], mla_k_g[l], rope)
        o_a = dense_attention(q_a, jnp.concatenate([kc_a, k_a], axis=1),
                              jnp.concatenate([vc_a, v_a], axis=1), MLA_SCALE)
        o_b = short_gated_conv(px[..., OFF_CB:OFF_CC], px[..., OFF_CC:OFF_CX], px[..., OFF_CX:OFF_NQ], conv_w[l])
        q_c = na_q(px[..., OFF_NQ:OFF_NK], na_q_g[l])
        k_c, v_c = na_kv(px[..., OFF_NK:], na_k_g[l])
        o_c = neighbourhood_attention(q_c, k_c, v_c, kc_c, vc_c, na_rpb[l])
        mix = jnp.concatenate([o_a.reshape(b, s, -1), o_b, o_c.reshape(b, s, -1)], axis=-1)
        h = h + gx1 * (mix @ w_out[l])
        h = h + gx2 * expert_choice_moe(modulate(h, norm2_g[l], sx2, cx2),
                                        w_router[l], ex_gate[l], ex_up[l], ex_down[l])

        if not last:
            n_ctx = hc.shape[1]
            qc_a = mla_q(pc[..., OFF_CQ:OFF_CKV], mla_qa_g[l], mla_w_uq[l], mla_q_g[l], None)
            oc_a = dense_attention(qc_a, kc_a, vc_a, MLA_SCALE)
            oc_b = short_gated_conv(pc[..., OFF_CB:OFF_CC], pc[..., OFF_CC:OFF_CX], pc[..., OFF_CX:OFF_NQ], conv_w[l])
            qc_c = na_q(pc[..., OFF_NQ:OFF_NK], na_q_g[l])
            oc_c = dense_attention(qc_c, kc_c, vc_c, NA_SCALE)
            mix_c = jnp.concatenate([oc_a.reshape(b, n_ctx, -1), oc_b, oc_c.reshape(b, n_ctx, -1)], axis=-1)
            hc = hc + gc1 * (mix_c @ w_out[l])
            hc = hc + gc2 * expert_choice_moe(modulate(hc, norm2_g[l], sc2, cc2),
                                              w_router[l], ex_gate[l], ex_up[l], ex_down[l])
    return h
```

```python
import functools

import jax
import jax.numpy as jnp
import numpy as np
from jax import lax
from jax.experimental import pallas as pl
from jax.experimental.pallas import tpu as pltpu

F32 = jnp.float32
BF16 = jnp.bfloat16
HIGHEST = lax.Precision.HIGHEST

GRID_W = 64
HEAD_DIM = 128
MLA_ROPE = 64
ROPE_F = MLA_ROPE // 4
ROPE_BASE = 10000.0
CAP_FACTOR = 2
EPS = 1e-6
N_MOD = 6
NEG = -1e30

V7X_LANES = 128
V7X_MXU = 256
V7X_VMEM_LIMIT = 56 * 1024 * 1024
NA_QROWS = 4
NA_KROWS = 12


def _pick(n, target, mult):
    best = None
    for d in range(mult, min(n, target) + 1, mult):
        if n % d == 0:
            best = d
    assert best is not None, (n, target, mult)
    return best


def _cparams(sem):
    return pltpu.CompilerParams(dimension_semantics=sem, vmem_limit_bytes=V7X_VMEM_LIMIT)


def _nt_dot(a, b):
    return lax.dot_general(a, b, (((1,), (1,)), ((), ())), preferred_element_type=F32)


def _rms(x, eps):
    return x * lax.rsqrt(jnp.mean(x * x, axis=-1, keepdims=True) + eps)


def _ada_kernel(cond_ref, down_ref, up_ref, bias_ref, out_ref, t_ref):
    @pl.when(pl.program_id(1) == 0)
    def _():
        c = cond_ref[...]
        s = c / (1.0 + jnp.exp(-c))
        t_ref[...] = jnp.dot(s, down_ref[0], precision=HIGHEST, preferred_element_type=F32)

    out_ref[0] = jnp.dot(t_ref[...], up_ref[0], precision=HIGHEST, preferred_element_type=F32) + bias_ref[0]


def _ada_mod(cond, down, up, bias):
    depth, d, rk = down.shape
    n = up.shape[-1]
    tn = _pick(n, 4096, V7X_LANES)
    rows = cond.shape[0]
    return pl.pallas_call(
        _ada_kernel,
        grid=(depth, n // tn),
        in_specs=[
            pl.BlockSpec((rows, d), lambda l, j: (0, 0)),
            pl.BlockSpec((1, d, rk), lambda l, j: (l, 0, 0)),
            pl.BlockSpec((1, rk, tn), lambda l, j: (l, 0, j)),
            pl.BlockSpec((1, 1, tn), lambda l, j: (l, 0, j)),
        ],
        out_specs=pl.BlockSpec((1, rows, tn), lambda l, j: (l, 0, j)),
        out_shape=jax.ShapeDtypeStruct((depth, rows, n), F32),
        scratch_shapes=[pltpu.VMEM((rows, rk), F32)],
        compiler_params=_cparams(("arbitrary", "arbitrary")),
        name="ada_mod",
    )(cond, down, up, bias.reshape(depth, 1, n))


def _modulated_norm(x, g, lat_shift, lat_scale, ctx_shift, ctx_scale, is_ctx, eps):
    y = _rms(x, eps) * g
    shift = jnp.where(is_ctx, ctx_shift, lat_shift)
    scale = jnp.where(is_ctx, ctx_scale, lat_scale)
    return y * (1.0 + scale) + shift


def _row_is_ctx(tile_in_batch, tm, row0, rows, s_rows):
    r = tile_in_batch * tm + row0 + lax.broadcasted_iota(jnp.int32, (rows, 1), 0)
    return r >= s_rows


def _proj_in_kernel(h_ref, g_ref, lat_ref, ctx_ref, w_ref, o_ref, nx_ref, *, tm, tpb, s_rows, chunk, eps):
    @pl.when(pl.program_id(1) == 0)
    def _():
        tile = pl.program_id(0) % tpb

        def body(c, carry):
            r0 = pl.multiple_of(c * chunk, chunk)
            x = h_ref[pl.ds(r0, chunk), :]
            is_ctx = _row_is_ctx(tile, tm, r0, chunk, s_rows)
            y = _modulated_norm(x, g_ref[...], lat_ref[0, 0:1, :], lat_ref[0, 1:2, :],
                                ctx_ref[0:1, :], ctx_ref[1:2, :], is_ctx, eps)
            nx_ref[pl.ds(r0, chunk), :] = y.astype(BF16)
            return carry

        lax.fori_loop(0, tm // chunk, body, 0)

    o_ref[...] = jnp.dot(nx_ref[...], w_ref[...], preferred_element_type=F32).astype(o_ref.dtype)


def _proj_in(h, g, lat_mod, ctx_mod, w, *, t_rows, s_rows):
    r, d = h.shape
    n = w.shape[1]
    tm = _pick(t_rows, 768, 128)
    tn = _pick(n, 512, V7X_LANES)
    tpb = t_rows // tm
    kern = functools.partial(_proj_in_kernel, tm=tm, tpb=tpb, s_rows=s_rows, chunk=_pick(tm, 128, 8), eps=EPS)
    return pl.pallas_call(
        kern,
        grid=(r // tm, n // tn),
        in_specs=[
            pl.BlockSpec((tm, d), lambda i, j: (i, 0)),
            pl.BlockSpec((1, d), lambda i, j: (0, 0)),
            pl.BlockSpec((1, 2, d), lambda i, j: (i // tpb, 0, 0)),
            pl.BlockSpec((2, d), lambda i, j: (0, 0)),
            pl.BlockSpec((d, tn), lambda i, j: (0, j)),
        ],
        out_specs=pl.BlockSpec((tm, tn), lambda i, j: (i, j)),
        out_shape=jax.ShapeDtypeStruct((r, n), BF16),
        scratch_shapes=[pltpu.VMEM((tm, d), BF16)],
        compiler_params=_cparams(("parallel", "arbitrary")),
        name="proj_in",
    )(h, g, lat_mod, ctx_mod, w)


def _rope(r, cos, sin):
    return r * cos + pltpu.roll(r, V7X_LANES // 2, 1) * sin


def _mla_prep_kernel(cq_ref, ckv_ref, kr_ref, gqa_ref, gkva_ref, wq_ref, wk_ref, wv_ref, gq_ref, gk_ref,
                     cos_ref, sin_ref, q_ref, k_ref, v_ref, *, heads, qk_dim, scale, eps):
    hd, hp = HEAD_DIM, V7X_MXU
    cqn = (_rms(cq_ref[...].astype(F32), eps) * gqa_ref[...]).astype(BF16)
    ckvn = (_rms(ckv_ref[...].astype(F32), eps) * gkva_ref[...]).astype(BF16)
    kr = kr_ref[...].astype(F32)
    cos, sin = cos_ref[...], sin_ref[...]
    kr2 = jnp.sum(kr * kr, axis=-1, keepdims=True)
    v_ref[...] = jnp.dot(ckvn, wv_ref[...], preferred_element_type=F32).astype(v_ref.dtype)
    for h in range(heads):
        a = jnp.dot(cqn, wq_ref[:, h * hp:(h + 1) * hp], preferred_element_type=F32)
        inv = lax.rsqrt(jnp.sum(a * a, axis=-1, keepdims=True) / qk_dim + eps)
        an = a * inv * gq_ref[...]
        q_ref[:, h * hp:h * hp + hd] = (an[:, :hd] * scale).astype(q_ref.dtype)
        q_ref[:, h * hp + hd:(h + 1) * hp] = (_rope(an[:, hd:], cos, sin) * scale).astype(q_ref.dtype)
        kn = jnp.dot(ckvn, wk_ref[:, h * hd:(h + 1) * hd], preferred_element_type=F32)
        invk = lax.rsqrt((jnp.sum(kn * kn, axis=-1, keepdims=True) + kr2) / qk_dim + eps)
        k_ref[:, h * hp:h * hp + hd] = (kn * invk * gk_ref[:, :hd]).astype(k_ref.dtype)
        k_ref[:, h * hp + hd:(h + 1) * hp] = _rope(kr * invk * gk_ref[:, hd:], cos, sin).astype(k_ref.dtype)


def _mla_prep(px, lay, gqa, gkva, wq, wk, wv, gq, gk, cos, sin, *, t_rows, heads, qk_dim, scale):
    r = px.shape[0]
    tm = _pick(t_rows, 384, 128)
    tpb = t_rows // tm
    qr, kvr = gqa.shape[1], gkva.shape[1]
    kern = functools.partial(_mla_prep_kernel, heads=heads, qk_dim=qk_dim, scale=scale, eps=EPS)
    full = lambda a: pl.BlockSpec(a.shape, lambda i: (0,) * a.ndim)
    return pl.pallas_call(
        kern,
        grid=(r // tm,),
        in_specs=[
            pl.BlockSpec((tm, qr), lambda i: (i, lay["cq"] // qr)),
            pl.BlockSpec((tm, kvr), lambda i: (i, lay["ckv"] // kvr)),
            pl.BlockSpec((tm, V7X_LANES), lambda i: (i, lay["kr"] // V7X_LANES)),
            full(gqa), full(gkva), full(wq), full(wk), full(wv), full(gq), full(gk),
            pl.BlockSpec((tm, V7X_LANES), lambda i: (i % tpb, 0)),
            pl.BlockSpec((tm, V7X_LANES), lambda i: (i % tpb, 0)),
        ],
        out_specs=[
            pl.BlockSpec((tm, heads * V7X_MXU), lambda i: (i, 0)),
            pl.BlockSpec((tm, heads * V7X_MXU), lambda i: (i, 0)),
            pl.BlockSpec((tm, heads * HEAD_DIM), lambda i: (i, 0)),
        ],
        out_shape=[
            jax.ShapeDtypeStruct((r, heads * V7X_MXU), BF16),
            jax.ShapeDtypeStruct((r, heads * V7X_MXU), BF16),
            jax.ShapeDtypeStruct((r, heads * HEAD_DIM), BF16),
        ],
        compiler_params=_cparams(("parallel",)),
        name="mla_prep",
    )(px, px, px, gqa, gkva, wq, wk, wv, gq, gk, cos, sin)


def _flash_kernel(q_ref, k_ref, v_ref, o_ref, m_ref, l_ref, acc_ref, *, tk, nkv):
    q = q_ref[0]
    m_ref[...] = jnp.full(m_ref.shape, -jnp.inf, F32)
    l_ref[...] = jnp.zeros(l_ref.shape, F32)
    acc_ref[...] = jnp.zeros(acc_ref.shape, F32)

    def body(j, carry):
        off = pl.multiple_of(j * tk, tk)
        s = _nt_dot(q, k_ref[0, pl.ds(off, tk), :])
        m_prev = m_ref[...]
        m_new = jnp.maximum(m_prev, jnp.max(s, axis=-1, keepdims=True))
        alpha = jnp.exp(m_prev - m_new)
        p = jnp.exp(s - m_new)
        l_ref[...] = alpha * l_ref[...] + jnp.sum(p, axis=-1, keepdims=True)
        acc_ref[...] = alpha * acc_ref[...] + jnp.dot(p.astype(BF16), v_ref[0, pl.ds(off, tk), :],
                                                      preferred_element_type=F32)
        m_ref[...] = m_new
        return carry

    lax.fori_loop(0, nkv, body, 0)
    o_ref[0] = (acc_ref[...] / l_ref[...]).astype(o_ref.dtype)


def _flash(q, k, v, *, heads, q_row0, n_q, kv_row0, n_kv, tq, tk, name):
    b = q.shape[0]
    dk, dv = V7X_MXU, HEAD_DIM
    assert q_row0 % tq == 0 and n_q % tq == 0 and kv_row0 % n_kv == 0 and n_kv % tk == 0
    kern = functools.partial(_flash_kernel, tk=tk, nkv=n_kv // tk)
    return pl.pallas_call(
        kern,
        grid=(b, heads, n_q // tq),
        in_specs=[
            pl.BlockSpec((1, tq, dk), lambda bi, h, i: (bi, q_row0 // tq + i, h)),
            pl.BlockSpec((1, n_kv, dk), lambda bi, h, i: (bi, kv_row0 // n_kv, h)),
            pl.BlockSpec((1, n_kv, dv), lambda bi, h, i: (bi, kv_row0 // n_kv, h)),
        ],
        out_specs=pl.BlockSpec((1, tq, dv), lambda bi, h, i: (bi, i, h)),
        out_shape=jax.ShapeDtypeStruct((b, n_q, heads * dv), BF16),
        scratch_shapes=[pltpu.VMEM((tq, 1), F32), pltpu.VMEM((tq, 1), F32), pltpu.VMEM((tq, dv), F32)],
        compiler_params=_cparams(("parallel", "parallel", "arbitrary")),
        name=name,
    )(q, k, v)


def _na_table_kernel(rpb_ref, tbl_ref, *, win_r, win_c, w):
    h = pl.program_id(0)
    q = lax.broadcasted_iota(jnp.int32, (w, 2 * w), 0)
    lane = lax.broadcasted_iota(jnp.int32, (w, 2 * w), 1)
    c0 = jnp.clip(q - win_c // 2, 0, w - win_c)
    n_d = 2 * win_r - 1
    for side in range(2):
        kc = lane - side * w
        inside = (kc >= c0) & (kc < c0 + win_c)
        own_half = (kc >= 0) & (kc < w)
        rel = kc - q + (win_c - 1)
        for d in range(n_d):
            acc = jnp.full((w, 2 * w), NEG, F32)
            for dc in range(2 * win_c - 1):
                acc = jnp.where(inside & (rel == dc), rpb_ref[h, d, dc], acc)
            tbl_ref[0, side, d] = jnp.where(own_half, acc, 0.0)
        tbl_ref[0, side, n_d] = jnp.where(own_half, NEG, 0.0)


def _na_table(rpb):
    heads, n_d, n_c = rpb.shape
    win_r, win_c = (n_d + 1) // 2, (n_c + 1) // 2
    kern = functools.partial(_na_table_kernel, win_r=win_r, win_c=win_c, w=GRID_W)
    return pl.pallas_call(
        kern,
        grid=(heads,),
        in_specs=[pl.BlockSpec(memory_space=pltpu.SMEM)],
        out_specs=pl.BlockSpec((1, 2, n_d + 1, GRID_W, 2 * GRID_W), lambda h: (h, 0, 0, 0, 0)),
        out_shape=jax.ShapeDtypeStruct((heads, 2, n_d + 1, GRID_W, 2 * GRID_W), F32),
        compiler_params=_cparams(("arbitrary",)),
        name="na_table",
    )(rpb)


def _na_kernel(q_ref, k_ref, v_ref, gq_ref, gk_ref, tbl_ref, o_ref, kn_ref, bias_ref, *,
               s_rows, l_rows, grid_rows, win_r, scale, eps, chunk):
    w = GRID_W
    t_rows = s_rows + l_rows
    nq, nk = NA_QROWS * w, NA_KROWS * w

    def kn_body(c, carry):
        r0 = pl.multiple_of(c * chunk, chunk)
        kk = k_ref[0, pl.ds(r0, chunk), :].astype(F32)
        kn_ref[pl.ds(r0, chunk), :] = (_rms(kk, eps) * gk_ref[...]).astype(BF16)
        return carry

    lax.fori_loop(0, t_rows // chunk, kn_body, 0)
    kc = kn_ref[s_rows:t_rows, :]
    vc = v_ref[0, s_rows:t_rows, :]

    def norm_q(x):
        return (_rms(x.astype(F32), eps) * gq_ref[...] * scale).astype(BF16)

    def softmax_pv(parts):
        m = functools.reduce(jnp.maximum, [jnp.max(s, axis=-1, keepdims=True) for s, _ in parts])
        ps = [jnp.exp(s - m) for s, _ in parts]
        den = functools.reduce(lambda a, b: a + b, [jnp.sum(p, axis=-1, keepdims=True) for p in ps])
        num = functools.reduce(lambda a, b: a + b,
                               [jnp.dot(p.astype(BF16), vv, preferred_element_type=F32)
                                for p, (_, vv) in zip(ps, parts)])
        return num / den

    def blk_body(i, carry):
        q0 = pl.multiple_of(i * nq, nq)
        ks = jnp.clip(i * NA_QROWS - win_r // 2, 0, grid_rows - NA_KROWS)
        k0 = pl.multiple_of(ks * w, w)
        qn = norm_q(q_ref[0, pl.ds(q0, nq), :])
        for qr in range(NA_QROWS):
            r = i * NA_QROWS + qr
            r0 = jnp.clip(r - win_r // 2, 0, grid_rows - win_r)
            for p in range(NA_KROWS // 2):
                ds = []
                for side in range(2):
                    kr = ks + 2 * p + side
                    ok = (kr >= r0) & (kr < r0 + win_r)
                    ds.append(jnp.where(ok, kr - r + (win_r - 1), 2 * win_r - 1))
                bias_ref[qr * w:(qr + 1) * w, p * 2 * w:(p + 1) * 2 * w] = (
                    tbl_ref[0, 0, ds[0]] + tbl_ref[0, 1, ds[1]])
        s_loc = _nt_dot(qn, kn_ref[pl.ds(k0, nk), :]) + bias_ref[...]
        s_ctx = _nt_dot(qn, kc)
        o = softmax_pv([(s_ctx, vc), (s_loc, v_ref[0, pl.ds(k0, nk), :])])
        o_ref[0, pl.ds(q0, nq), :] = o.astype(o_ref.dtype)
        return carry

    lax.fori_loop(0, grid_rows // NA_QROWS, blk_body, 0)

    qc = norm_q(q_ref[0, s_rows:t_rows, :])
    o_ref[0, s_rows:t_rows, :] = softmax_pv([(_nt_dot(qc, kc), vc)]).astype(o_ref.dtype)


def _na_attention(px3, lay, gq, gk, tbl, *, heads, s_rows, l_rows, win_r, scale):
    b, t, _ = px3.shape
    hd = HEAD_DIM
    grid_rows = s_rows // GRID_W
    assert 2 * GRID_W == V7X_LANES and grid_rows % NA_QROWS == 0 and grid_rows >= NA_KROWS
    assert NA_KROWS >= NA_QROWS + win_r - 1 and NA_KROWS % 2 == 0
    kern = functools.partial(_na_kernel, s_rows=s_rows, l_rows=l_rows, grid_rows=grid_rows, win_r=win_r,
                             scale=scale, eps=EPS, chunk=_pick(t, 256, 16))
    col = lambda name: (lambda bi, h: (bi, 0, lay[name] // hd + h))
    return pl.pallas_call(
        kern,
        grid=(b, heads),
        in_specs=[
            pl.BlockSpec((1, t, hd), col("nq")),
            pl.BlockSpec((1, t, hd), col("nk")),
            pl.BlockSpec((1, t, hd), col("nv")),
            pl.BlockSpec((1, hd), lambda bi, h: (0, 0)),
            pl.BlockSpec((1, hd), lambda bi, h: (0, 0)),
            pl.BlockSpec((1,) + tbl.shape[1:], lambda bi, h: (h, 0, 0, 0, 0)),
        ],
        out_specs=pl.BlockSpec((1, t, hd), lambda bi, h: (bi, 0, h)),
        out_shape=jax.ShapeDtypeStruct((b, t, heads * hd), BF16),
        scratch_shapes=[pltpu.VMEM((t, hd), BF16), pltpu.VMEM((NA_QROWS * GRID_W, NA_KROWS * GRID_W), F32)],
        compiler_params=_cparams(("parallel", "parallel")),
        name="na_attn",
    )(px3, px3, px3, gq, gk, tbl)


def _conv_kernel(cb_ref, cc_ref, cx_ref, ccp_ref, cxp_ref, ccn_ref, cxn_ref, w_ref, o_ref, *,
                 tm, tpb, s_rows, t_rows, halo):
    u = cc_ref[...].astype(F32) * cx_ref[...].astype(F32)
    u_before = ccp_ref[halo - 1:halo, :].astype(F32) * cxp_ref[halo - 1:halo, :].astype(F32)
    u_after = ccn_ref[0:1, :].astype(F32) * cxn_ref[0:1, :].astype(F32)
    loc = lax.broadcasted_iota(jnp.int32, (tm, 1), 0)
    row = (pl.program_id(0) % tpb) * tm + loc
    first = (row == 0) | (row == s_rows)
    last = (row == s_rows - 1) | (row == t_rows - 1)
    u_prev = jnp.where(loc == 0, u_before, pltpu.roll(u, 1, 0))
    u_prev = jnp.where(first, 0.0, u_prev)
    u_next = jnp.where(loc == tm - 1, u_after, pltpu.roll(u, tm - 1, 0))
    u_next = jnp.where(last, 0.0, u_next)
    y = w_ref[0:1, :] * u_prev + w_ref[1:2, :] * u + w_ref[2:3, :] * u_next
    o_ref[...] = (cb_ref[...].astype(F32) * y).astype(o_ref.dtype)


def _gated_conv(px, lay, w, *, t_rows, s_rows):
    r = px.shape[0]
    c = w.shape[1]
    halo = 16
    tm = _pick(t_rows, 256, 128)
    tpb = t_rows // tm
    nb = r // halo
    kern = functools.partial(_conv_kernel, tm=tm, tpb=tpb, s_rows=s_rows, t_rows=t_rows, halo=halo)
    blk = lambda name: pl.BlockSpec((tm, c), lambda i: (i, lay[name] // c))
    prev = lambda name: pl.BlockSpec((halo, c), lambda i: (jnp.maximum(i * (tm // halo) - 1, 0), lay[name] // c))
    nxt = lambda name: pl.BlockSpec((halo, c), lambda i: (jnp.minimum((i + 1) * (tm // halo), nb - 1), lay[name] // c))
    return pl.pallas_call(
        kern,
        grid=(r // tm,),
        in_specs=[blk("cb"), blk("cc"), blk("cx"), prev("cc"), prev("cx"), nxt("cc"), nxt("cx"),
                  pl.BlockSpec(w.shape, lambda i: (0, 0))],
        out_specs=pl.BlockSpec((tm, c), lambda i: (i, 0)),
        out_shape=jax.ShapeDtypeStruct((r, c), BF16),
        compiler_params=_cparams(("parallel",)),
        name="gated_conv",
    )(px, px, px, px, px, px, px, w)


def _proj_out_kernel(oa_ref, ob_ref, oc_ref, wa_ref, wb_ref, wc_ref, h_ref, lat_ref, ctx_ref, o_ref, *,
                     tm, tpb, s_rows):
    y = jnp.dot(oa_ref[...], wa_ref[...], preferred_element_type=F32)
    y = y + jnp.dot(ob_ref[...], wb_ref[...], preferred_element_type=F32)
    y = y + jnp.dot(oc_ref[...], wc_ref[...], preferred_element_type=F32)
    is_ctx = _row_is_ctx(pl.program_id(0) % tpb, tm, 0, tm, s_rows)
    gate = jnp.where(is_ctx, ctx_ref[...], lat_ref[0])
    o_ref[...] = h_ref[...] + gate * y


def _proj_out(oa, ob, oc, wa, wb, wc, h, lat_gate, ctx_gate, *, t_rows, s_rows):
    r, d = h.shape
    tm = _pick(t_rows, 768, 128)
    tn = _pick(d, 1024, V7X_LANES)
    tpb = t_rows // tm
    kern = functools.partial(_proj_out_kernel, tm=tm, tpb=tpb, s_rows=s_rows)
    lhs = lambda a: pl.BlockSpec((tm, a.shape[1]), lambda i, j: (i, 0))
    rhs = lambda a: pl.BlockSpec((a.shape[0], tn), lambda i, j: (0, j))
    return pl.pallas_call(
        kern,
        grid=(r // tm, d // tn),
        in_specs=[lhs(oa), lhs(ob), lhs(oc), rhs(wa), rhs(wb), rhs(wc),
                  pl.BlockSpec((tm, tn), lambda i, j: (i, j)),
                  pl.BlockSpec((1, 1, tn), lambda i, j: (i // tpb, 0, j)),
                  pl.BlockSpec((1, tn), lambda i, j: (0, j))],
        out_specs=pl.BlockSpec((tm, tn), lambda i, j: (i, j)),
        out_shape=jax.ShapeDtypeStruct((r, d), F32),
        input_output_aliases={6: 0},
        compiler_params=_cparams(("parallel", "arbitrary")),
        name="proj_out",
    )(oa, ob, oc, wa, wb, wc, h, lat_gate, ctx_gate)


def _moe_norm_kernel(h_ref, g_ref, lat_ref, ctx_ref, wr_ref, hn_ref, aff_ref, *, tm, tpb, s_rows, n_exp, eps):
    is_ctx = _row_is_ctx(pl.program_id(0) % tpb, tm, 0, tm, s_rows)
    hn = _modulated_norm(h_ref[...], g_ref[...], lat_ref[0, 0:1, :], lat_ref[0, 1:2, :],
                         ctx_ref[0:1, :], ctx_ref[1:2, :], is_ctx, eps)
    hn_ref[...] = hn
    logits = jnp.dot(hn.astype(BF16), wr_ref[...], preferred_element_type=F32)
    lane = lax.broadcasted_iota(jnp.int32, logits.shape, 1)
    logits = jnp.where(lane < n_exp, logits, -jnp.inf)
    e = jnp.exp(logits - jnp.max(logits, axis=-1, keepdims=True))
    aff = e / jnp.sum(e, axis=-1, keepdims=True)
    aff_ref[0] = aff.T[:n_exp, :]


def _moe_norm(h, g, lat_mod, ctx_mod, wr, *, b, t_rows, s_rows, n_exp):
    r, d = h.shape
    tm = _pick(t_rows, 256, 128)
    tpb = t_rows // tm
    kern = functools.partial(_moe_norm_kernel, tm=tm, tpb=tpb, s_rows=s_rows, n_exp=n_exp, eps=EPS)
    return pl.pallas_call(
        kern,
        grid=(r // tm,),
        in_specs=[
            pl.BlockSpec((tm, d), lambda i: (i, 0)),
            pl.BlockSpec((1, d), lambda i: (0, 0)),
            pl.BlockSpec((1, 2, d), lambda i: (i // tpb, 0, 0)),
            pl.BlockSpec((2, d), lambda i: (0, 0)),
            pl.BlockSpec(wr.shape, lambda i: (0, 0)),
        ],
        out_specs=[pl.BlockSpec((tm, d), lambda i: (i, 0)),
                   pl.BlockSpec((1, n_exp, tm), lambda i: (i // tpb, 0, i % tpb))],
        out_shape=[jax.ShapeDtypeStruct((r, d), F32), jax.ShapeDtypeStruct((b, n_exp, t_rows), F32)],
        compiler_params=_cparams(("parallel",)),
        name="moe_norm",
    )(h, g, lat_mod, ctx_mod, wr)


def _topk_kernel(aff_ref, tri_ref, ones_ref, idx_ref, gate_ref, posm_ref, *, n, k, n_exp):
    ln = V7X_LANES
    nt = n // ln
    a = aff_ref[0]
    bits = pltpu.bitcast(a, jnp.int32)
    thr = jnp.zeros((n_exp, 1), jnp.int32)
    for bit in range(30, -1, -1):
        cand = thr | (1 << bit)
        cnt = jnp.sum((bits >= cand).astype(jnp.int32), axis=1, keepdims=True)
        thr = jnp.where(cnt >= k, cand, thr)
    gt = bits > thr
    eq = bits == thr

    def excl_prefix(x):
        xb = x.astype(BF16)
        run = jnp.zeros((n_exp, ln), F32)
        outs = []
        for t in range(nt):
            blk = xb[:, t * ln:(t + 1) * ln]
            outs.append(jnp.dot(blk, tri_ref[...], preferred_element_type=F32) + run)
            run = run + jnp.dot(blk, ones_ref[...], preferred_element_type=F32)
        return jnp.concatenate(outs, axis=1)

    need = (k - jnp.sum(gt.astype(jnp.int32), axis=1, keepdims=True)).astype(F32)
    eqf = jnp.where(eq, 1.0, 0.0)
    sel = gt | (eq & (excl_prefix(eqf) < need))
    pos = excl_prefix(jnp.where(sel, 1.0, 0.0))
    posm_ref[...] = jnp.where(sel, pos, -1.0)

    lane = lax.broadcasted_iota(jnp.int32, (8, ln), 1)
    lane_f = lane.astype(F32)

    def jb_body(jb, carry):
        slot = (jb * 8 + lax.broadcasted_iota(jnp.int32, (8, 1), 0)).astype(F32)
        out_i = jnp.zeros((8, ln), F32)
        out_g = jnp.zeros((8, ln), F32)
        for e in range(n_exp):
            acc_i = jnp.zeros((8, ln), F32)
            acc_g = jnp.zeros((8, ln), F32)
            for t in range(nt):
                hit = posm_ref[e:e + 1, t * ln:(t + 1) * ln] == slot
                acc_i = acc_i + jnp.where(hit, lane_f + float(t * ln), 0.0)
                acc_g = acc_g + jnp.where(hit, aff_ref[0, e:e + 1, t * ln:(t + 1) * ln], 0.0)
            out_i = jnp.where(lane == e, jnp.sum(acc_i, axis=1, keepdims=True), out_i)
            out_g = jnp.where(lane == e, jnp.sum(acc_g, axis=1, keepdims=True), out_g)
        r0 = pl.multiple_of(jb * 8, 8)
        idx_ref[0, pl.ds(r0, 8), :] = out_i.astype(jnp.int32)
        gate_ref[0, pl.ds(r0, 8), :] = out_g
        return carry

    lax.fori_loop(0, k // 8, jb_body, 0)


def _topk(aff, tri, ones, *, row0, n, k):
    b, n_exp, _ = aff.shape
    assert row0 % n == 0 and n % V7X_LANES == 0 and k % 8 == 0 and n_exp <= V7X_LANES
    kern = functools.partial(_topk_kernel, n=n, k=k, n_exp=n_exp)
    return pl.pallas_call(
        kern,
        grid=(b,),
        in_specs=[pl.BlockSpec((1, n_exp, n), lambda bi: (bi, 0, row0 // n)),
                  pl.BlockSpec(tri.shape, lambda bi: (0, 0)),
                  pl.BlockSpec(ones.shape, lambda bi: (0, 0))],
        out_specs=[pl.BlockSpec((1, k, V7X_LANES), lambda bi: (bi, 0, 0)),
                   pl.BlockSpec((1, k, V7X_LANES), lambda bi: (bi, 0, 0))],
        out_shape=[jax.ShapeDtypeStruct((b, k, V7X_LANES), jnp.int32),
                   jax.ShapeDtypeStruct((b, k, V7X_LANES), F32)],
        scratch_shapes=[pltpu.VMEM((n_exp, n), F32)],
        compiler_params=_cparams(("parallel",)),
        name="topk",
    )(aff, tri, ones)


def _expert_kernel(idx_ref, g_ref, lat_ref, ctx_ref, wg_ref, wu_ref, wd_ref, hn_hbm, h_hbm, out_hbm,
                   xe_ref, y_ref, sem, *, rows, cap_lat, dcols):
    del h_hbm
    f = pl.program_id(2)
    nf = pl.num_programs(2)

    def row_copy(src, dst, s):
        return pltpu.make_async_copy(src, dst, sem.at[s])

    def gather(src_hbm, s):
        def start(j, carry):
            row_copy(src_hbm.at[pl.ds(idx_ref[0, 0, 0, j], 1)], y_ref.at[pl.ds(j, 1)], s).start()
            return carry

        def wait(j, carry):
            row_copy(src_hbm.at[pl.ds(0, 1)], y_ref.at[pl.ds(0, 1)], s).wait()
            return carry

        lax.fori_loop(0, rows, start, 0)
        lax.fori_loop(0, rows, wait, 0)

    @pl.when(f == 0)
    def _():
        gather(hn_hbm, 0)
        xe_ref[...] = y_ref[...].astype(BF16)
        gather(out_hbm, 1)

    xe = xe_ref[...]
    a = jnp.dot(xe, wg_ref[0].astype(BF16), preferred_element_type=F32)
    u = jnp.dot(xe, wu_ref[0].astype(BF16), preferred_element_type=F32)
    hm = ((a / (1.0 + jnp.exp(-a))) * u).astype(BF16)
    g = g_ref[0, 0]
    d = y_ref.shape[1]
    for c in range(d // dcols):
        cs = slice(c * dcols, (c + 1) * dcols)
        y = jnp.dot(hm, wd_ref[0, :, cs].astype(BF16), preferred_element_type=F32) * g
        y_ref[0:cap_lat, cs] = y_ref[0:cap_lat, cs] + lat_ref[0, :, cs] * y[0:cap_lat]
        y_ref[cap_lat:rows, cs] = y_ref[cap_lat:rows, cs] + ctx_ref[:, cs] * y[cap_lat:rows]

    @pl.when(f == nf - 1)
    def _():
        def start(j, carry):
            row_copy(y_ref.at[pl.ds(j, 1)], out_hbm.at[pl.ds(idx_ref[0, 0, 0, j], 1)], 2).start()
            return carry

        def wait(j, carry):
            row_copy(y_ref.at[pl.ds(0, 1)], out_hbm.at[pl.ds(0, 1)], 2).wait()
            return carry

        lax.fori_loop(0, rows, start, 0)
        lax.fori_loop(0, rows, wait, 0)


def _experts(idx, gates, lat_gate, ctx_gate, wg, wu, wd, hn, h, *, cap_lat):
    n_exp, b, _, rows = idx.shape
    d, ff = wg.shape[1], wg.shape[2]
    fc = _pick(ff, 256, V7X_LANES)
    kern = functools.partial(_expert_kernel, rows=rows, cap_lat=cap_lat, dcols=_pick(d, 1024, V7X_LANES))
    return pl.pallas_call(
        kern,
        grid=(n_exp, b, ff // fc),
        in_specs=[
            pl.BlockSpec((1, 1, 1, rows), lambda e, bi, f: (e, bi, 0, 0), memory_space=pltpu.SMEM),
            pl.BlockSpec((1, 1, rows, 1), lambda e, bi, f: (e, bi, 0, 0)),
            pl.BlockSpec((1, 1, d), lambda e, bi, f: (bi, 0, 0)),
            pl.BlockSpec((1, d), lambda e, bi, f: (0, 0)),
            pl.BlockSpec((1, d, fc), lambda e, bi, f: (e, 0, f)),
            pl.BlockSpec((1, d, fc), lambda e, bi, f: (e, 0, f)),
            pl.BlockSpec((1, fc, d), lambda e, bi, f: (e, f, 0)),
            pl.BlockSpec(memory_space=pl.ANY),
            pl.BlockSpec(memory_space=pl.ANY),
        ],
        out_specs=pl.BlockSpec(memory_space=pl.ANY),
        out_shape=jax.ShapeDtypeStruct(h.shape, F32),
        scratch_shapes=[pltpu.VMEM((rows, d), BF16), pltpu.VMEM((rows, d), F32), pltpu.SemaphoreType.DMA((3,))],
        input_output_aliases={8: 0},
        compiler_params=_cparams(("arbitrary", "arbitrary", "arbitrary")),
        name="experts",
    )(idx, gates, lat_gate, ctx_gate, wg, wu, wd, hn, h)


def _in_layout(conv_ch, q_rank, kv_rank, na_cols):
    lay, off = {}, 0

    def put(name, width, align):
        nonlocal off
        off = -(-off // align) * align
        lay[name] = off
        off += width

    for name in ("cb", "cc", "cx"):
        put(name, conv_ch, conv_ch)
    put("cq", q_rank, q_rank)
    for name in ("nq", "nk", "nv"):
        put(name, na_cols, HEAD_DIM)
    put("ckv", kv_rank, kv_rank)
    put("kr", V7X_LANES, V7X_LANES)
    lay["n"] = -(-off // 512) * 512
    return lay


def _spread_rope(a):
    half = MLA_ROPE // 2
    z = jnp.zeros(a.shape[:-1] + (V7X_LANES // 2 - half,), a.dtype)
    return jnp.concatenate([a[..., :half], z, a[..., half:], z], axis=-1)


def _pad_heads(a, heads, nope):
    a = a.reshape(a.shape[:-1] + (heads, nope + MLA_ROPE))
    a = jnp.concatenate([a[..., :nope], _spread_rope(a[..., nope:])], axis=-1)
    return a.reshape(a.shape[:-2] + (heads * V7X_MXU,))


def _rope_tables(s_rows, l_rows):
    pos = jnp.arange(s_rows, dtype=jnp.int32)
    row = (pos // GRID_W).astype(F32)
    col = (pos % GRID_W).astype(F32)
    inv = jnp.power(ROPE_BASE, -jnp.arange(ROPE_F, dtype=F32) / ROPE_F)
    ang = jnp.concatenate([row[:, None] * inv, col[:, None] * inv], axis=1)
    cos = jnp.concatenate([jnp.cos(ang), jnp.ones((l_rows, 2 * ROPE_F), F32)], axis=0)
    sin = jnp.concatenate([jnp.sin(ang), jnp.zeros((l_rows, 2 * ROPE_F), F32)], axis=0)
    return _spread_rope(jnp.concatenate([cos, cos], axis=1)), _spread_rope(jnp.concatenate([-sin, sin], axis=1))


def kernel(x, c, ctx, c_ctx, ada_down, ada_up, ada_bias, norm1_g, w_in, mla_qa_g, mla_kva_g, mla_w_uq, mla_w_ukv, mla_q_g, mla_k_g, conv_w, na_q_g, na_k_g, na_rpb, w_out, norm2_g, w_router, ex_gate, ex_up, ex_down):
    b, s_rows, d = x.shape
    l_rows = ctx.shape[1]
    t_rows = s_rows + l_rows
    depth = w_in.shape[0]
    q_rank, kv_rank = mla_qa_g.shape[1], mla_kva_g.shape[1]
    qk_dim = mla_q_g.shape[1]
    nope = qk_dim - MLA_ROPE
    assert nope == HEAD_DIM
    mla_heads = mla_w_uq.shape[2] // qk_dim
    conv_ch = conv_w.shape[2]
    na_heads = na_rpb.shape[1]
    na_cols = na_heads * HEAD_DIM
    win_r = (na_rpb.shape[2] + 1) // 2
    n_exp = w_router.shape[2]
    cap_lat = CAP_FACTOR * s_rows // n_exp
    cap_ctx = CAP_FACTOR * l_rows // n_exp
    lay = _in_layout(conv_ch, q_rank, kv_rank, na_cols)
    mla_cols = mla_heads * HEAD_DIM

    o_cq = 0
    o_ckv = o_cq + q_rank
    o_kr = o_ckv + kv_rank
    o_cb = o_kr + MLA_ROPE
    o_cc = o_cb + conv_ch
    o_cx = o_cc + conv_ch
    o_nq = o_cx + conv_ch
    o_nk = o_nq + na_cols
    o_nv = o_nk + na_cols

    n_cond = -(-(b + 1) // 8) * 8
    cond = jnp.zeros((n_cond, d), F32).at[:b].set(c).at[b].set(c_ctx)
    mods = _ada_mod(cond, ada_down, ada_up, ada_bias).reshape(depth, n_cond, N_MOD, d)

    cos_t, sin_t = _rope_tables(s_rows, l_rows)
    tri = jnp.triu(jnp.ones((V7X_LANES, V7X_LANES), BF16), 1)
    ones = jnp.ones((V7X_LANES, V7X_LANES), BF16)
    tok0 = (jnp.arange(b, dtype=jnp.int32) * t_rows)[None, :, None]

    h = jnp.concatenate([x, ctx], axis=1).reshape(b * t_rows, d)

    for l in range(depth):
        m = mods[l]
        lat = lambda i, j: jnp.stack([m[:b, i], m[:b, j]], axis=1)
        cxm = lambda i, j: jnp.stack([m[b, i], m[b, j]], axis=0)

        wl = w_in[l]
        sec ={"cb": wl[:, o_cb:o_cc], "cc": wl[:, o_cc:o_cx], "cx": wl[:, o_cx:o_nq], "cq": wl[:, o_cq:o_ckv],
               "nq": wl[:, o_nq:o_nk], "nk": wl[:, o_nk:o_nv], "nv": wl[:, o_nv:o_nv + na_cols],
               "ckv": wl[:, o_ckv:o_kr], "kr": _spread_rope(wl[:, o_kr:o_cb])}
        w_p = jnp.zeros((d, lay["n"]), BF16)
        for name, blk in sec.items():
            w_p = lax.dynamic_update_slice(w_p, blk.astype(BF16), (0, lay[name]))
        wq = _pad_heads(mla_w_uq[l], mla_heads, nope).astype(BF16)
        wkv = mla_w_ukv[l].reshape(kv_rank, mla_heads, 2 * HEAD_DIM)
        wk = wkv[:, :, :HEAD_DIM].reshape(kv_rank, mla_cols).astype(BF16)
        wv = wkv[:, :, HEAD_DIM:].reshape(kv_rank, mla_cols).astype(BF16)
        gq = _pad_heads(mla_q_g[l][None, :], 1, nope)
        gk = _pad_heads(mla_k_g[l][None, :], 1, nope)
        wo = w_out[l].astype(BF16)
        wo_a, wo_b, wo_c = wo[:mla_cols], wo[mla_cols:mla_cols + conv_ch], wo[mla_cols + conv_ch:]
        wr = jnp.zeros((d, V7X_LANES), BF16).at[:, :n_exp].set(w_router[l].astype(BF16))

        px = _proj_in(h, norm1_g[l][None, :], lat(0, 1), cxm(0, 1), w_p, t_rows=t_rows, s_rows=s_rows)
        q, k, v = _mla_prep(px, lay, mla_qa_g[l][None, :], mla_kva_g[l][None, :], wq, wk, wv, gq, gk, cos_t, sin_t,
                            t_rows=t_rows, heads=mla_heads, qk_dim=qk_dim, scale=qk_dim ** -0.5)
        q3, k3, v3 = (a.reshape(b, t_rows, -1) for a in (q, k, v))
        o_a = jnp.concatenate([
            _flash(q3, k3, v3, heads=mla_heads, q_row0=0, n_q=s_rows, kv_row0=0, n_kv=t_rows,
                   tq=_pick(s_rows, 512, 128), tk=_pick(t_rows, 768, 128), name="mla_attn"),
            _flash(q3, k3, v3, heads=mla_heads, q_row0=s_rows, n_q=l_rows, kv_row0=s_rows, n_kv=l_rows,
                   tq=l_rows, tk=l_rows, name="mla_attn_ctx")], axis=1)
        o_b = _gated_conv(px, lay, conv_w[l], t_rows=t_rows, s_rows=s_rows)
        o_c = _na_attention(px.reshape(b, t_rows, -1), lay, na_q_g[l][None, :], na_k_g[l][None, :],
                            _na_table(na_rpb[l]), heads=na_heads, s_rows=s_rows, l_rows=l_rows, win_r=win_r,
                            scale=HEAD_DIM ** -0.5)
        h = _proj_out(o_a.reshape(b * t_rows, -1), o_b, o_c.reshape(b * t_rows, -1), wo_a, wo_b, wo_c, h,
                      m[:b, 2][:, None, :], m[b, 2][None, :], t_rows=t_rows, s_rows=s_rows)

        hn, aff = _moe_norm(h, norm2_g[l][None, :], lat(3, 4), cxm(3, 4), wr, b=b, t_rows=t_rows, s_rows=s_rows,
                            n_exp=n_exp)
        i_lat, g_lat = _topk(aff, tri, ones, row0=0, n=s_rows, k=cap_lat)
        i_ctx, g_ctx = _topk(aff, tri, ones, row0=s_rows, n=l_rows, k=cap_ctx)
        em = lambda a: jnp.transpose(a[:, :, :n_exp], (2, 0, 1))
        idx = jnp.concatenate([em(i_lat) + tok0, em(i_ctx) + tok0 + s_rows], axis=2)
        gates = jnp.concatenate([em(g_lat), em(g_ctx)], axis=2)
        h = _experts(idx[:, :, None, :], gates[..., None], m[:b, 5][:, None, :], m[b, 5][None, :],
                     ex_gate[l], ex_up[l], ex_down[l], hn, h, cap_lat=cap_lat)

    return h.reshape(b, t_rows, d)[:, :s_rows]
```

```python
import functools

import jax
import jax.numpy as jnp
import numpy as np
from jax import lax
from jax.experimental import pallas as pl
from jax.experimental.pallas import tpu as pltpu

F32 = jnp.float32
BF16 = jnp.bfloat16
HIGHEST = lax.Precision.HIGHEST

GRID_W = 64
HEAD_DIM = 128
MLA_ROPE = 64
ROPE_F = MLA_ROPE // 4
ROPE_BASE = 10000.0
CAP_FACTOR = 2
EPS = 1e-6
N_MOD = 6
LOG2E = 1.4426950408889634
NEG = -1e30

V7X_LANES = 128
V7X_MXU = 256
V7X_VMEM_LIMIT = 56 * 1024 * 1024
NA_QROWS = 4
NA_KROWS = 12
VT_TILE = 256


def _pick(n, target, mult):
    best = None
    for d in range(mult, min(n, target) + 1, mult):
        if n % d == 0:
            best = d
    assert best is not None, (n, target, mult)
    return best


def _cparams(sem):
    return pltpu.CompilerParams(dimension_semantics=sem, vmem_limit_bytes=V7X_VMEM_LIMIT)


def _nt_dot(a, b):
    return lax.dot_general(a, b, (((1,), (1,)), ((), ())), preferred_element_type=F32)


def _rms(x, eps):
    return x * lax.rsqrt(jnp.mean(x * x, axis=-1, keepdims=True) + eps)


def _ada_kernel(cond_ref, down_ref, up_ref, bias_ref, out_ref, t_ref):
    @pl.when(pl.program_id(1) == 0)
    def _():
        c = cond_ref[...]
        s = c / (1.0 + jnp.exp(-c))
        t_ref[...] = jnp.dot(s, down_ref[0], precision=HIGHEST, preferred_element_type=F32)

    out_ref[0] = jnp.dot(t_ref[...], up_ref[0], precision=HIGHEST, preferred_element_type=F32) + bias_ref[0]


def _ada_mod(cond, down, up, bias):
    depth, d, rk = down.shape
    n = up.shape[-1]
    tn = _pick(n, 4096, V7X_LANES)
    rows = cond.shape[0]
    return pl.pallas_call(
        _ada_kernel,
        grid=(depth, n // tn),
        in_specs=[
            pl.BlockSpec((rows, d), lambda l, j: (0, 0)),
            pl.BlockSpec((1, d, rk), lambda l, j: (l, 0, 0)),
            pl.BlockSpec((1, rk, tn), lambda l, j: (l, 0, j)),
            pl.BlockSpec((1, 1, tn), lambda l, j: (l, 0, j)),
        ],
        out_specs=pl.BlockSpec((1, rows, tn), lambda l, j: (l, 0, j)),
        out_shape=jax.ShapeDtypeStruct((depth, rows, n), F32),
        scratch_shapes=[pltpu.VMEM((rows, rk), F32)],
        compiler_params=_cparams(("arbitrary", "arbitrary")),
        name="ada_mod",
    )(cond, down, up, bias.reshape(depth, 1, n))


def _modulated_norm(x, g, lat_shift, lat_scale, ctx_shift, ctx_scale, is_ctx, eps):
    y = _rms(x, eps) * g
    shift = jnp.where(is_ctx, ctx_shift, lat_shift)
    scale = jnp.where(is_ctx, ctx_scale, lat_scale)
    return y * (1.0 + scale) + shift


def _row_is_ctx(tile_in_batch, tm, row0, rows, s_rows):
    r = tile_in_batch * tm + row0 + lax.broadcasted_iota(jnp.int32, (rows, 1), 0)
    return r >= s_rows


def _proj_in_kernel(h_ref, g_ref, lat_ref, ctx_ref, w_ref, o_ref, nx_ref, *, tm, tpb, s_rows, chunk, eps):
    @pl.when(pl.program_id(1) == 0)
    def _():
        tile = pl.program_id(0) % tpb

        def body(c, carry):
            r0 = pl.multiple_of(c * chunk, chunk)
            x = h_ref[pl.ds(r0, chunk), :]
            is_ctx = _row_is_ctx(tile, tm, r0, chunk, s_rows)
            y = _modulated_norm(x, g_ref[...], lat_ref[0, 0:1, :], lat_ref[0, 1:2, :],
                                ctx_ref[0:1, :], ctx_ref[1:2, :], is_ctx, eps)
            nx_ref[pl.ds(r0, chunk), :] = y.astype(BF16)
            return carry

        lax.fori_loop(0, tm // chunk, body, 0)

    o_ref[...] = jnp.dot(nx_ref[...], w_ref[...], preferred_element_type=F32).astype(o_ref.dtype)


def _proj_in(h, g, lat_mod, ctx_mod, w, *, t_rows, s_rows):
    r, d = h.shape
    n = w.shape[1]
    tm = _pick(t_rows, 768, 128)
    tn = _pick(n, 512, V7X_LANES)
    tpb = t_rows // tm
    kern = functools.partial(_proj_in_kernel, tm=tm, tpb=tpb, s_rows=s_rows, chunk=_pick(tm, 128, 8), eps=EPS)
    return pl.pallas_call(
        kern,
        grid=(r // tm, n // tn),
        in_specs=[
            pl.BlockSpec((tm, d), lambda i, j: (i, 0)),
            pl.BlockSpec((1, d), lambda i, j: (0, 0)),
            pl.BlockSpec((1, 2, d), lambda i, j: (i // tpb, 0, 0)),
            pl.BlockSpec((2, d), lambda i, j: (0, 0)),
            pl.BlockSpec((d, tn), lambda i, j: (0, j)),
        ],
        out_specs=pl.BlockSpec((tm, tn), lambda i, j: (i, j)),
        out_shape=jax.ShapeDtypeStruct((r, n), BF16),
        scratch_shapes=[pltpu.VMEM((tm, d), BF16)],
        compiler_params=_cparams(("parallel", "arbitrary")),
        name="proj_in",
    )(h, g, lat_mod, ctx_mod, w)


def _rope(r, cos, sin):
    return r * cos + pltpu.roll(r, V7X_LANES // 2, 1) * sin


def _mla_prep_kernel(cq_ref, ckv_ref, kr_ref, gqa_ref, gkva_ref, wq_ref, wk_ref, wv_ref, gq_ref, gk_ref,
                     cos_ref, sin_ref, q_ref, k_ref, vt_ref, *, heads, qk_dim, scale, eps):
    hd, hp = HEAD_DIM, V7X_MXU
    cqn = (_rms(cq_ref[...].astype(F32), eps) * gqa_ref[...]).astype(BF16)
    ckvn = (_rms(ckv_ref[...].astype(F32), eps) * gkva_ref[...]).astype(BF16)
    kr = kr_ref[...].astype(F32)
    cos, sin = cos_ref[...], sin_ref[...]
    kr2 = jnp.sum(kr * kr, axis=-1, keepdims=True)
    vt_ref[0, 0] = _nt_dot(wv_ref[...], ckvn).astype(vt_ref.dtype)
    for h in range(heads):
        a = jnp.dot(cqn, wq_ref[:, h * hp:(h + 1) * hp], preferred_element_type=F32)
        inv = lax.rsqrt(jnp.sum(a * a, axis=-1, keepdims=True) / qk_dim + eps)
        an = a * inv * gq_ref[...]
        q_ref[:, h * hp:h * hp + hd] = (an[:, :hd] * scale).astype(q_ref.dtype)
        q_ref[:, h * hp + hd:(h + 1) * hp] = (_rope(an[:, hd:], cos, sin) * scale).astype(q_ref.dtype)
        kn = jnp.dot(ckvn, wk_ref[:, h * hd:(h + 1) * hd], preferred_element_type=F32)
        invk = lax.rsqrt((jnp.sum(kn * kn, axis=-1, keepdims=True) + kr2) / qk_dim + eps)
        k_ref[:, h * hp:h * hp + hd] = (kn * invk * gk_ref[:, :hd]).astype(k_ref.dtype)
        k_ref[:, h * hp + hd:(h + 1) * hp] = _rope(kr * invk * gk_ref[:, hd:], cos, sin).astype(k_ref.dtype)


def _mla_prep(px, lay, gqa, gkva, wq, wk, wv, gq, gk, cos, sin, *, t_rows, heads, qk_dim, scale):
    r = px.shape[0]
    tm = VT_TILE
    tpb = t_rows // tm
    qr, kvr = gqa.shape[1], gkva.shape[1]
    kern = functools.partial(_mla_prep_kernel, heads=heads, qk_dim=qk_dim, scale=scale, eps=EPS)
    full = lambda a: pl.BlockSpec(a.shape, lambda i: (0,) * a.ndim)
    return pl.pallas_call(
        kern,
        grid=(r // tm,),
        in_specs=[
            pl.BlockSpec((tm, qr), lambda i: (i, lay["cq"] // qr)),
            pl.BlockSpec((tm, kvr), lambda i: (i, lay["ckv"] // kvr)),
            pl.BlockSpec((tm, V7X_LANES), lambda i: (i, lay["kr"] // V7X_LANES)),
            full(gqa), full(gkva), full(wq), full(wk), full(wv), full(gq), full(gk),
            pl.BlockSpec((tm, V7X_LANES), lambda i: (i % tpb, 0)),
            pl.BlockSpec((tm, V7X_LANES), lambda i: (i % tpb, 0)),
        ],
        out_specs=[
            pl.BlockSpec((tm, heads * V7X_MXU), lambda i: (i, 0)),
            pl.BlockSpec((tm, heads * V7X_MXU), lambda i: (i, 0)),
            pl.BlockSpec((1, 1, heads * HEAD_DIM, tm), lambda i: (i // tpb, i % tpb, 0, 0)),
        ],
        out_shape=[
            jax.ShapeDtypeStruct((r, heads * V7X_MXU), BF16),
            jax.ShapeDtypeStruct((r, heads * V7X_MXU), BF16),
            jax.ShapeDtypeStruct((r // t_rows, tpb, heads * HEAD_DIM, tm), BF16),
        ],
        compiler_params=_cparams(("parallel",)),
        name="mla_prep",
    )(px, px, px, gqa, gkva, wq, wk, wv, gq, gk, cos, sin)


def _flash_kernel(q_ref, k_ref, vt_ref, o_ref, s0_ref, s1_ref, mx0_ref, mx1_ref, m_ref, l_ref, acc_ref, *, tk, nkv):
    m_ref[...] = jnp.full(m_ref.shape, -jnp.inf, F32)
    l_ref[...] = jnp.zeros(l_ref.shape, F32)
    acc_ref[...] = jnp.zeros(acc_ref.shape, F32)
    vpt = tk // VT_TILE

    def scores(j, s_ref, mx_ref):
        off = pl.multiple_of(j * tk, tk)
        st = _nt_dot(k_ref[0, pl.ds(off, tk), :], q_ref[0])
        s_ref[...] = st
        mx_ref[...] = jnp.max(st, axis=0, keepdims=True)

    def accumulate(j, s_ref, mx_ref):
        m_prev = m_ref[...]
        m_new = jnp.maximum(m_prev, mx_ref[...])
        alpha = jnp.exp2(m_prev - m_new)
        p = jnp.exp2(s_ref[...] - m_new)
        l_ref[...] = alpha * l_ref[...] + jnp.sum(p, axis=0, keepdims=True)
        pb = p.astype(BF16)
        pv = jnp.dot(vt_ref[0, j * vpt], pb[0:VT_TILE], preferred_element_type=F32)
        for c in range(1, vpt):
            pv = pv + jnp.dot(vt_ref[0, j * vpt + c], pb[c * VT_TILE:(c + 1) * VT_TILE],
                              preferred_element_type=F32)
        acc_ref[...] = alpha * acc_ref[...] + pv
        m_ref[...] = m_new

    scores(0, s0_ref, mx0_ref)
    npairs = (nkv - 1) // 2

    def body(jj, carry):
        j = 2 * jj
        scores(j + 1, s1_ref, mx1_ref)
        accumulate(j, s0_ref, mx0_ref)
        scores(j + 2, s0_ref, mx0_ref)
        accumulate(j + 1, s1_ref, mx1_ref)
        return carry

    lax.fori_loop(0, npairs, body, 0)
    if (nkv - 1) % 2 == 0:
        accumulate(nkv - 1, s0_ref, mx0_ref)
    else:
        scores(nkv - 1, s1_ref, mx1_ref)
        accumulate(nkv - 2, s0_ref, mx0_ref)
        accumulate(nkv - 1, s1_ref, mx1_ref)
    o_ref[0] = (acc_ref[...] / l_ref[...]).T.astype(o_ref.dtype)


def _flash(q, k, vt, *, heads, q_row0, n_q, kv_row0, n_kv, tq, tk, name):
    b = q.shape[0]
    dk, dv = V7X_MXU, HEAD_DIM
    assert q_row0 % tq == 0 and n_q % tq == 0 and kv_row0 % n_kv == 0 and n_kv % tk == 0
    assert tk % VT_TILE == 0
    kern = functools.partial(_flash_kernel, tk=tk, nkv=n_kv // tk)
    return pl.pallas_call(
        kern,
        grid=(b, heads, n_q // tq),
        in_specs=[
            pl.BlockSpec((1, tq, dk), lambda bi, h, i: (bi, q_row0 // tq + i, h)),
            pl.BlockSpec((1, n_kv, dk), lambda bi, h, i: (bi, kv_row0 // n_kv, h)),
            pl.BlockSpec((1, n_kv // VT_TILE, dv, VT_TILE), lambda bi, h, i: (bi, kv_row0 // n_kv, h, 0)),
        ],
        out_specs=pl.BlockSpec((1, tq, dv), lambda bi, h, i: (bi, i, h)),
        out_shape=jax.ShapeDtypeStruct((b, n_q, heads * dv), BF16),
        scratch_shapes=[pltpu.VMEM((tk, tq), F32), pltpu.VMEM((tk, tq), F32),
                        pltpu.VMEM((1, tq), F32), pltpu.VMEM((1, tq), F32),
                        pltpu.VMEM((1, tq), F32), pltpu.VMEM((1, tq), F32), pltpu.VMEM((dv, tq), F32)],
        compiler_params=_cparams(("parallel", "parallel", "arbitrary")),
        name=name,
    )(q, k, vt)


def _na_table_kernel(rpb_ref, tbl_ref, *, win_r, win_c, w):
    h = pl.program_id(0)
    q = lax.broadcasted_iota(jnp.int32, (w, 2 * w), 0)
    lane = lax.broadcasted_iota(jnp.int32, (w, 2 * w), 1)
    c0 = jnp.clip(q - win_c // 2, 0, w - win_c)
    n_d = 2 * win_r - 1
    for side in range(2):
        kc = lane - side * w
        inside = (kc >= c0) & (kc < c0 + win_c)
        own_half = (kc >= 0) & (kc < w)
        rel = kc - q + (win_c - 1)
        for d in range(n_d):
            acc = jnp.full((w, 2 * w), NEG, F32)
            for dc in range(2 * win_c - 1):
                acc = jnp.where(inside & (rel == dc), rpb_ref[h, d, dc], acc)
            tbl_ref[0, side, d] = jnp.where(own_half, acc, 0.0)
        tbl_ref[0, side, n_d] = jnp.where(own_half, NEG, 0.0)


def _na_table(rpb):
    heads, n_d, n_c = rpb.shape
    win_r, win_c = (n_d + 1) // 2, (n_c + 1) // 2
    kern = functools.partial(_na_table_kernel, win_r=win_r, win_c=win_c, w=GRID_W)
    return pl.pallas_call(
        kern,
        grid=(heads,),
        in_specs=[pl.BlockSpec(memory_space=pltpu.SMEM)],
        out_specs=pl.BlockSpec((1, 2, n_d + 1, GRID_W, 2 * GRID_W), lambda h: (h, 0, 0, 0, 0)),
        out_shape=jax.ShapeDtypeStruct((heads, 2, n_d + 1, GRID_W, 2 * GRID_W), F32),
        compiler_params=_cparams(("arbitrary",)),
        name="na_table",
    )(rpb)


def _na_kernel(q_ref, k_ref, v_ref, gq_ref, gk_ref, tbl_ref, o_ref, kn_ref, bias_ref, *,
               s_rows, l_rows, grid_rows, win_r, scale, eps, chunk):
    w = GRID_W
    t_rows = s_rows + l_rows
    nq, nk = NA_QROWS * w, NA_KROWS * w

    def kn_body(c, carry):
        r0 = pl.multiple_of(c * chunk, chunk)
        kk = k_ref[0, pl.ds(r0, chunk), :].astype(F32)
        kn_ref[pl.ds(r0, chunk), :] = (_rms(kk, eps) * gk_ref[...]).astype(BF16)
        return carry

    lax.fori_loop(0, t_rows // chunk, kn_body, 0)
    kc = kn_ref[s_rows:t_rows, :]
    vc = v_ref[0, s_rows:t_rows, :]

    def norm_q(x):
        return (_rms(x.astype(F32), eps) * gq_ref[...] * scale).astype(BF16)

    def softmax_pv(parts):
        m = functools.reduce(jnp.maximum, [jnp.max(s, axis=-1, keepdims=True) for s, _ in parts])
        ps = [jnp.exp(s - m) for s, _ in parts]
        den = functools.reduce(lambda a, b: a + b, [jnp.sum(p, axis=-1, keepdims=True) for p in ps])
        num = functools.reduce(lambda a, b: a + b,
                               [jnp.dot(p.astype(BF16), vv, preferred_element_type=F32)
                                for p, (_, vv) in zip(ps, parts)])
        return num / den

    def blk_body(i, carry):
        q0 = pl.multiple_of(i * nq, nq)
        ks = jnp.clip(i * NA_QROWS - win_r // 2, 0, grid_rows - NA_KROWS)
        k0 = pl.multiple_of(ks * w, w)
        qn = norm_q(q_ref[0, pl.ds(q0, nq), :])
        for qr in range(NA_QROWS):
            r = i * NA_QROWS + qr
            r0 = jnp.clip(r - win_r // 2, 0, grid_rows - win_r)
            for p in range(NA_KROWS // 2):
                ds = []
                for side in range(2):
                    kr = ks + 2 * p + side
                    ok = (kr >= r0) & (kr < r0 + win_r)
                    ds.append(jnp.where(ok, kr - r + (win_r - 1), 2 * win_r - 1))
                bias_ref[qr * w:(qr + 1) * w, p * 2 * w:(p + 1) * 2 * w] = (
                    tbl_ref[0, 0, ds[0]] + tbl_ref[0, 1, ds[1]])
        s_loc = _nt_dot(qn, kn_ref[pl.ds(k0, nk), :]) + bias_ref[...]
        s_ctx = _nt_dot(qn, kc)
        o = softmax_pv([(s_ctx, vc), (s_loc, v_ref[0, pl.ds(k0, nk), :])])
        o_ref[0, pl.ds(q0, nq), :] = o.astype(o_ref.dtype)
        return carry

    lax.fori_loop(0, grid_rows // NA_QROWS, blk_body, 0)

    qc = norm_q(q_ref[0, s_rows:t_rows, :])
    o_ref[0, s_rows:t_rows, :] = softmax_pv([(_nt_dot(qc, kc), vc)]).astype(o_ref.dtype)


def _na_attention(px3, lay, gq, gk, tbl, *, heads, s_rows, l_rows, win_r, scale):
    b, t, _ = px3.shape
    hd = HEAD_DIM
    grid_rows = s_rows // GRID_W
    assert 2 * GRID_W == V7X_LANES and grid_rows % NA_QROWS == 0 and grid_rows >= NA_KROWS
    assert NA_KROWS >= NA_QROWS + win_r - 1 and NA_KROWS % 2 == 0
    kern = functools.partial(_na_kernel, s_rows=s_rows, l_rows=l_rows, grid_rows=grid_rows, win_r=win_r,
                             scale=scale, eps=EPS, chunk=_pick(t, 256, 16))
    col = lambda name: (lambda bi, h: (bi, 0, lay[name] // hd + h))
    return pl.pallas_call(
        kern,
        grid=(b, heads),
        in_specs=[
            pl.BlockSpec((1, t, hd), col("nq")),
            pl.BlockSpec((1, t, hd), col("nk")),
            pl.BlockSpec((1, t, hd), col("nv")),
            pl.BlockSpec((1, hd), lambda bi, h: (0, 0)),
            pl.BlockSpec((1, hd), lambda bi, h: (0, 0)),
            pl.BlockSpec((1,) + tbl.shape[1:], lambda bi, h: (h, 0, 0, 0, 0)),
        ],
        out_specs=pl.BlockSpec((1, t, hd), lambda bi, h: (bi, 0, h)),
        out_shape=jax.ShapeDtypeStruct((b, t, heads * hd), BF16),
        scratch_shapes=[pltpu.VMEM((t, hd), BF16), pltpu.VMEM((NA_QROWS * GRID_W, NA_KROWS * GRID_W), F32)],
        compiler_params=_cparams(("parallel", "parallel")),
        name="na_attn",
    )(px3, px3, px3, gq, gk, tbl)


def _conv_kernel(cb_ref, cc_ref, cx_ref, ccp_ref, cxp_ref, ccn_ref, cxn_ref, w_ref, o_ref, *,
                 tm, tpb, s_rows, t_rows, halo):
    u = cc_ref[...].astype(F32) * cx_ref[...].astype(F32)
    u_before = ccp_ref[halo - 1:halo, :].astype(F32) * cxp_ref[halo - 1:halo, :].astype(F32)
    u_after = ccn_ref[0:1, :].astype(F32) * cxn_ref[0:1, :].astype(F32)
    loc = lax.broadcasted_iota(jnp.int32, (tm, 1), 0)
    row = (pl.program_id(0) % tpb) * tm + loc
    first = (row == 0) | (row == s_rows)
    last = (row == s_rows - 1) | (row == t_rows - 1)
    u_prev = jnp.where(loc == 0, u_before, pltpu.roll(u, 1, 0))
    u_prev = jnp.where(first, 0.0, u_prev)
    u_next = jnp.where(loc == tm - 1, u_after, pltpu.roll(u, tm - 1, 0))
    u_next = jnp.where(last, 0.0, u_next)
    y = w_ref[0:1, :] * u_prev + w_ref[1:2, :] * u + w_ref[2:3, :] * u_next
    o_ref[...] = (cb_ref[...].astype(F32) * y).astype(o_ref.dtype)


def _gated_conv(px, lay, w, *, t_rows, s_rows):
    r = px.shape[0]
    c = w.shape[1]
    halo = 16
    tm = _pick(t_rows, 256, 128)
    tpb = t_rows // tm
    nb = r // halo
    kern = functools.partial(_conv_kernel, tm=tm, tpb=tpb, s_rows=s_rows, t_rows=t_rows, halo=halo)
    blk = lambda name: pl.BlockSpec((tm, c), lambda i: (i, lay[name] // c))
    prev = lambda name: pl.BlockSpec((halo, c), lambda i: (jnp.maximum(i * (tm // halo) - 1, 0), lay[name] // c))
    nxt = lambda name: pl.BlockSpec((halo, c), lambda i: (jnp.minimum((i + 1) * (tm // halo), nb - 1), lay[name] // c))
    return pl.pallas_call(
        kern,
        grid=(r // tm,),
        in_specs=[blk("cb"), blk("cc"), blk("cx"), prev("cc"), prev("cx"), nxt("cc"), nxt("cx"),
                  pl.BlockSpec(w.shape, lambda i: (0, 0))],
        out_specs=pl.BlockSpec((tm, c), lambda i: (i, 0)),
        out_shape=jax.ShapeDtypeStruct((r, c), BF16),
        compiler_params=_cparams(("parallel",)),
        name="gated_conv",
    )(px, px, px, px, px, px, px, w)


def _proj_out_kernel(oa_ref, ob_ref, oc_ref, wa_ref, wb_ref, wc_ref, h_ref, lat_ref, ctx_ref, o_ref, *,
                     tm, tpb, s_rows):
    y = jnp.dot(oa_ref[...], wa_ref[...], preferred_element_type=F32)
    y = y + jnp.dot(ob_ref[...], wb_ref[...], preferred_element_type=F32)
    y = y + jnp.dot(oc_ref[...], wc_ref[...], preferred_element_type=F32)
    is_ctx = _row_is_ctx(pl.program_id(0) % tpb, tm, 0, tm, s_rows)
    gate = jnp.where(is_ctx, ctx_ref[...], lat_ref[0])
    o_ref[...] = h_ref[...] + gate * y


def _proj_out(oa, ob, oc, wa, wb, wc, h, lat_gate, ctx_gate, *, t_rows, s_rows):
    r, d = h.shape
    tm = _pick(t_rows, 768, 128)
    tn = _pick(d, 1024, V7X_LANES)
    tpb = t_rows // tm
    kern = functools.partial(_proj_out_kernel, tm=tm, tpb=tpb, s_rows=s_rows)
    lhs = lambda a: pl.BlockSpec((tm, a.shape[1]), lambda i, j: (i, 0))
    rhs = lambda a: pl.BlockSpec((a.shape[0], tn), lambda i, j: (0, j))
    return pl.pallas_call(
        kern,
        grid=(r // tm, d // tn),
        in_specs=[lhs(oa), lhs(ob), lhs(oc), rhs(wa), rhs(wb), rhs(wc),
                  pl.BlockSpec((tm, tn), lambda i, j: (i, j)),
                  pl.BlockSpec((1, 1, tn), lambda i, j: (i // tpb, 0, j)),
                  pl.BlockSpec((1, tn), lambda i, j: (0, j))],
        out_specs=pl.BlockSpec((tm, tn), lambda i, j: (i, j)),
        out_shape=jax.ShapeDtypeStruct((r, d), F32),
        input_output_aliases={6: 0},
        compiler_params=_cparams(("parallel", "arbitrary")),
        name="proj_out",
    )(oa, ob, oc, wa, wb, wc, h, lat_gate, ctx_gate)


def _moe_norm_kernel(h_ref, g_ref, lat_ref, ctx_ref, wr_ref, hn_ref, aff_ref, *, tm, tpb, s_rows, n_exp, eps):
    is_ctx = _row_is_ctx(pl.program_id(0) % tpb, tm, 0, tm, s_rows)
    hn = _modulated_norm(h_ref[...], g_ref[...], lat_ref[0, 0:1, :], lat_ref[0, 1:2, :],
                         ctx_ref[0:1, :], ctx_ref[1:2, :], is_ctx, eps)
    hb = hn.astype(BF16)
    half = hn.shape[1] // 2
    hf = hb.astype(F32)
    hn_ref[...] = pltpu.bitcast(hf[:, :half], jnp.uint32) | (pltpu.bitcast(hf[:, half:], jnp.uint32) >> 16)
    logits = jnp.dot(hb, wr_ref[...], preferred_element_type=F32)
    lane = lax.broadcasted_iota(jnp.int32, logits.shape, 1)
    logits = jnp.where(lane < n_exp, logits, -jnp.inf)
    e = jnp.exp(logits - jnp.max(logits, axis=-1, keepdims=True))
    aff = e / jnp.sum(e, axis=-1, keepdims=True)
    aff_ref[0] = aff.T[:n_exp, :]


def _moe_norm(h, g, lat_mod, ctx_mod, wr, *, b, t_rows, s_rows, n_exp):
    r, d = h.shape
    tm = _pick(t_rows, 256, 128)
    tpb = t_rows // tm
    kern = functools.partial(_moe_norm_kernel, tm=tm, tpb=tpb, s_rows=s_rows, n_exp=n_exp, eps=EPS)
    return pl.pallas_call(
        kern,
        grid=(r // tm,),
        in_specs=[
            pl.BlockSpec((tm, d), lambda i: (i, 0)),
            pl.BlockSpec((1, d), lambda i: (0, 0)),
            pl.BlockSpec((1, 2, d), lambda i: (i // tpb, 0, 0)),
            pl.BlockSpec((2, d), lambda i: (0, 0)),
            pl.BlockSpec(wr.shape, lambda i: (0, 0)),
        ],
        out_specs=[pl.BlockSpec((tm, d // 2), lambda i: (i, 0)),
                   pl.BlockSpec((1, n_exp, tm), lambda i: (i // tpb, 0, i % tpb))],
        out_shape=[jax.ShapeDtypeStruct((r, d // 2), jnp.uint32), jax.ShapeDtypeStruct((b, n_exp, t_rows), F32)],
        compiler_params=_cparams(("parallel",)),
        name="moe_norm",
    )(h, g, lat_mod, ctx_mod, wr)


def _topk_kernel(aff_ref, tri_ref, ones_ref, idx_ref, gate_ref, posm_ref, affs_ref, *, n, k, n_exp):
    ln = V7X_LANES
    nt = n // ln
    a = aff_ref[0]
    bits = pltpu.bitcast(a, jnp.int32)
    thr = jnp.zeros((n_exp, 1), jnp.int32)
    for bit in range(30, -1, -1):
        cand = thr | (1 << bit)
        cnt = jnp.sum((bits >= cand).astype(jnp.int32), axis=1, keepdims=True)
        thr = jnp.where(cnt >= k, cand, thr)
    gt = bits > thr
    eq = bits == thr

    def excl_prefix(x):
        xb = x.astype(BF16)
        run = jnp.zeros((n_exp, ln), F32)
        outs = []
        for t in range(nt):
            blk = xb[:, t * ln:(t + 1) * ln]
            outs.append(jnp.dot(blk, tri_ref[...], preferred_element_type=F32) + run)
            run = run + jnp.dot(blk, ones_ref[...], preferred_element_type=F32)
        return jnp.concatenate(outs, axis=1)

    need = (k - jnp.sum(gt.astype(jnp.int32), axis=1, keepdims=True)).astype(F32)
    eqf = jnp.where(eq, 1.0, 0.0)
    sel = gt | (eq & (excl_prefix(eqf) < need))
    pos = excl_prefix(jnp.where(sel, 1.0, 0.0))
    posm = jnp.where(sel, pos, -1.0)
    for e in range(n_exp):
        posm_ref[e] = posm[e:e + 1, :]
        affs_ref[e] = a[e:e + 1, :]

    lane = lax.broadcasted_iota(jnp.int32, (8, ln), 1)
    lane_f = lane.astype(F32)

    def jb_body(jb, carry):
        slot = (jb * 8 + lax.broadcasted_iota(jnp.int32, (8, 1), 0)).astype(F32)
        def e_body(e, outs):
            acc_i = jnp.zeros((8, ln), F32)
            acc_g = jnp.zeros((8, ln), F32)
            for t in range(nt):
                hit = posm_ref[e, :, t * ln:(t + 1) * ln] == slot
                acc_i = acc_i + jnp.where(hit, lane_f + float(t * ln), 0.0)
                acc_g = acc_g + jnp.where(hit, affs_ref[e, :, t * ln:(t + 1) * ln], 0.0)
            return (jnp.where(lane == e, jnp.sum(acc_i, axis=1, keepdims=True), outs[0]),
                    jnp.where(lane == e, jnp.sum(acc_g, axis=1, keepdims=True), outs[1]))

        out_i, out_g = lax.fori_loop(0, n_exp, e_body, (jnp.zeros((8, ln), F32), jnp.zeros((8, ln), F32)))
        r0 = pl.multiple_of(jb * 8, 8)
        idx_ref[0, pl.ds(r0, 8), :] = out_i.astype(jnp.int32)
        gate_ref[0, pl.ds(r0, 8), :] = out_g
        return carry

    lax.fori_loop(0, k // 8, jb_body, 0)


def _topk(aff, tri, ones, *, row0, n, k):
    b, n_exp, _ = aff.shape
    assert row0 % n == 0 and n % V7X_LANES == 0 and k % 8 == 0 and n_exp <= V7X_LANES
    kern = functools.partial(_topk_kernel, n=n, k=k, n_exp=n_exp)
    return pl.pallas_call(
        kern,
        grid=(b,),
        in_specs=[pl.BlockSpec((1, n_exp, n), lambda bi: (bi, 0, row0 // n)),
                  pl.BlockSpec(tri.shape, lambda bi: (0, 0)),
                  pl.BlockSpec(ones.shape, lambda bi: (0, 0))],
        out_specs=[pl.BlockSpec((1, k, V7X_LANES), lambda bi: (bi, 0, 0)),
                   pl.BlockSpec((1, k, V7X_LANES), lambda bi: (bi, 0, 0))],
        out_shape=[jax.ShapeDtypeStruct((b, k, V7X_LANES), jnp.int32),
                   jax.ShapeDtypeStruct((b, k, V7X_LANES), F32)],
        scratch_shapes=[pltpu.VMEM((n_exp, 1, n), F32), pltpu.VMEM((n_exp, 1, n), F32)],
        compiler_params=_cparams(("parallel",)),
        name="topk",
    )(aff, tri, ones)


ROW_UNROLL = 8
SEM_HN, SEM_H, SEM_OUT = 0, 2, 3


def _expert_kernel(idx_ref, g_ref, lat_ref, ctx_ref, wg_ref, wu_ref, wd_ref, hn_hbm, h_hbm, out_hbm,
                   xe_ref, y_ref, st_ref, sem, *, rows, cap_lat, dcols, nchunk):
    del h_hbm
    e, bi, f = pl.program_id(0), pl.program_id(1), pl.program_id(2)
    ne, nb, nf = pl.num_programs(0), pl.num_programs(1), pl.num_programs(2)
    ch = rows // nchunk
    half = xe_ref.shape[1] // 2

    def issue_rows(n_rows, make):
        def trip(jj, carry):
            for u in range(ROW_UNROLL):
                make(jj * ROW_UNROLL + u).start()
            return carry

        lax.fori_loop(0, n_rows // ROW_UNROLL, trip, 0)

    def hn_row(c, j):
        return pltpu.make_async_copy(hn_hbm.at[pl.ds(idx_ref[0, 0, 0, c * ch + j], 1)],
                                     st_ref.at[c % 2, pl.ds(j, 1)], sem.at[SEM_HN + c % 2])

    def hn_chunk_all(c):
        return pltpu.make_async_copy(hn_hbm.at[pl.ds(0, ch)], st_ref.at[c % 2], sem.at[SEM_HN + c % 2])

    def h_row(j):
        return pltpu.make_async_copy(out_hbm.at[pl.ds(idx_ref[0, 0, 0, j], 1)], y_ref.at[pl.ds(j, 1)],
                                     sem.at[SEM_H])

    def out_row(j):
        return pltpu.make_async_copy(y_ref.at[pl.ds(j, 1)], out_hbm.at[pl.ds(idx_ref[0, 0, 0, j], 1)],
                                     sem.at[SEM_OUT])

    def all_rows(s):
        return pltpu.make_async_copy(out_hbm.at[pl.ds(0, rows)], y_ref, sem.at[s])

    @pl.when(f == 0)
    def _():
        issue_rows(ch, functools.partial(hn_row, 0))

        @pl.when((e > 0) | (bi > 0))
        def _():
            all_rows(SEM_OUT).wait()

        issue_rows(rows, h_row)
        for c in range(nchunk):
            if c + 1 < nchunk:
                issue_rows(ch, functools.partial(hn_row, c + 1))
            hn_chunk_all(c).wait()
            p = st_ref[c % 2]
            hi = pltpu.bitcast(p & jnp.uint32(0xFFFF0000), F32)
            lo = pltpu.bitcast(p << 16, F32)
            xe_ref[c * ch:(c + 1) * ch, :half] = hi.astype(BF16)
            xe_ref[c * ch:(c + 1) * ch, half:] = lo.astype(BF16)

    xe = xe_ref[...]
    a = jnp.dot(xe, wg_ref[0, 0].astype(BF16), preferred_element_type=F32)
    u = jnp.dot(xe, wu_ref[0, 0].astype(BF16), preferred_element_type=F32)
    hm = ((a / (1.0 + jnp.exp(-a))) * u).astype(BF16)

    @pl.when(f == 0)
    def _():
        all_rows(SEM_H).wait()

    g = g_ref[0, 0]
    d = y_ref.shape[1]
    for c in range(d // dcols):
        cs = slice(c * dcols, (c + 1) * dcols)
        y = jnp.dot(hm, wd_ref[0, 0, :, cs].astype(BF16), preferred_element_type=F32) * g
        y_ref[0:cap_lat, cs] = y_ref[0:cap_lat, cs] + lat_ref[0, :, cs] * y[0:cap_lat]
        y_ref[cap_lat:rows, cs] = y_ref[cap_lat:rows, cs] + ctx_ref[:, cs] * y[cap_lat:rows]

    @pl.when(f == nf - 1)
    def _():
        issue_rows(rows, out_row)

        @pl.when((e == ne - 1) & (bi == nb - 1))
        def _():
            all_rows(SEM_OUT).wait()


def _experts(idx, gates, lat_gate, ctx_gate, wg, wu, wd, layer, hn, h, *, cap_lat):
    n_exp, b, _, rows = idx.shape
    d, ff = wg.shape[2], wg.shape[3]
    fc = _pick(ff, 256, V7X_LANES)
    nchunk = max(n for n in range(1, 7) if rows % (16 * n) == 0)
    assert rows % 16 == 0 and hn.shape[1] * 2 == d
    kern = functools.partial(_expert_kernel, rows=rows, cap_lat=cap_lat, dcols=_pick(d, 1024, V7X_LANES),
                             nchunk=nchunk)
    return pl.pallas_call(
        kern,
        grid=(n_exp, b, ff // fc),
        in_specs=[
            pl.BlockSpec((1, 1, 1, rows), lambda e, bi, f: (e, bi, 0, 0), memory_space=pltpu.SMEM),
            pl.BlockSpec((1, 1, rows, 1), lambda e, bi, f: (e, bi, 0, 0)),
            pl.BlockSpec((1, 1, d), lambda e, bi, f: (bi, 0, 0)),
            pl.BlockSpec((1, d), lambda e, bi, f: (0, 0)),
            pl.BlockSpec((1, 1, d, fc), lambda e, bi, f: (layer, e, 0, f)),
            pl.BlockSpec((1, 1, d, fc), lambda e, bi, f: (layer, e, 0, f)),
            pl.BlockSpec((1, 1, fc, d), lambda e, bi, f: (layer, e, f, 0)),
            pl.BlockSpec(memory_space=pl.ANY),
            pl.BlockSpec(memory_space=pl.ANY),
        ],
        out_specs=pl.BlockSpec(memory_space=pl.ANY),
        out_shape=jax.ShapeDtypeStruct(h.shape, F32),
        scratch_shapes=[pltpu.VMEM((rows, d), BF16), pltpu.VMEM((rows, d), F32),
                        pltpu.VMEM((2, rows // nchunk, d // 2), jnp.uint32), pltpu.SemaphoreType.DMA((4,))],
        input_output_aliases={8: 0},
        compiler_params=_cparams(("arbitrary", "arbitrary", "arbitrary")),
        name="experts",
    )(idx, gates, lat_gate, ctx_gate, wg, wu, wd, hn, h)


def _in_layout(conv_ch, q_rank, kv_rank, na_cols):
    lay, off = {}, 0

    def put(name, width, align):
        nonlocal off
        off = -(-off // align) * align
        lay[name] = off
        off += width

    for name in ("cb", "cc", "cx"):
        put(name, conv_ch, conv_ch)
    put("cq", q_rank, q_rank)
    for name in ("nq", "nk", "nv"):
        put(name, na_cols, HEAD_DIM)
    put("ckv", kv_rank, kv_rank)
    put("kr", V7X_LANES, V7X_LANES)
    lay["n"] = -(-off // 512) * 512
    return lay


def _spread_rope(a):
    half = MLA_ROPE // 2
    z = jnp.zeros(a.shape[:-1] + (V7X_LANES // 2 - half,), a.dtype)
    return jnp.concatenate([a[..., :half], z, a[..., half:], z], axis=-1)


def _pad_heads(a, heads, nope):
    a = a.reshape(a.shape[:-1] + (heads, nope + MLA_ROPE))
    a = jnp.concatenate([a[..., :nope], _spread_rope(a[..., nope:])], axis=-1)
    return a.reshape(a.shape[:-2] + (heads * V7X_MXU,))


def _rope_tables(s_rows, l_rows):
    pos = jnp.arange(s_rows, dtype=jnp.int32)
    row = (pos // GRID_W).astype(F32)
    col = (pos % GRID_W).astype(F32)
    inv = jnp.power(ROPE_BASE, -jnp.arange(ROPE_F, dtype=F32) / ROPE_F)
    ang = jnp.concatenate([row[:, None] * inv, col[:, None] * inv], axis=1)
    cos = jnp.concatenate([jnp.cos(ang), jnp.ones((l_rows, 2 * ROPE_F), F32)], axis=0)
    sin = jnp.concatenate([jnp.sin(ang), jnp.zeros((l_rows, 2 * ROPE_F), F32)], axis=0)
    return _spread_rope(jnp.concatenate([cos, cos], axis=1)), _spread_rope(jnp.concatenate([-sin, sin], axis=1))


def kernel(x, c, ctx, c_ctx, ada_down, ada_up, ada_bias, norm1_g, w_in, mla_qa_g, mla_kva_g, mla_w_uq, mla_w_ukv, mla_q_g, mla_k_g, conv_w, na_q_g, na_k_g, na_rpb, w_out, norm2_g, w_router, ex_gate, ex_up, ex_down):
    b, s_rows, d = x.shape
    l_rows = ctx.shape[1]
    t_rows = s_rows + l_rows
    depth = w_in.shape[0]
    q_rank, kv_rank = mla_qa_g.shape[1], mla_kva_g.shape[1]
    qk_dim = mla_q_g.shape[1]
    nope = qk_dim - MLA_ROPE
    assert nope == HEAD_DIM
    mla_heads = mla_w_uq.shape[2] // qk_dim
    conv_ch = conv_w.shape[2]
    na_heads = na_rpb.shape[1]
    na_cols = na_heads * HEAD_DIM
    win_r = (na_rpb.shape[2] + 1) // 2
    n_exp = w_router.shape[2]
    cap_lat = CAP_FACTOR * s_rows // n_exp
    cap_ctx = CAP_FACTOR * l_rows // n_exp
    lay = _in_layout(conv_ch, q_rank, kv_rank, na_cols)
    mla_cols = mla_heads * HEAD_DIM

    o_cq = 0
    o_ckv = o_cq + q_rank
    o_kr = o_ckv + kv_rank
    o_cb = o_kr + MLA_ROPE
    o_cc = o_cb + conv_ch
    o_cx = o_cc + conv_ch
    o_nq = o_cx + conv_ch
    o_nk = o_nq + na_cols
    o_nv = o_nk + na_cols

    n_cond = -(-(b + 1) // 8) * 8
    cond = jnp.zeros((n_cond, d), F32).at[:b].set(c).at[b].set(c_ctx)
    mods = _ada_mod(cond, ada_down, ada_up, ada_bias).reshape(depth, n_cond, N_MOD, d)

    cos_t, sin_t = _rope_tables(s_rows, l_rows)
    tri = jnp.triu(jnp.ones((V7X_LANES, V7X_LANES), BF16), 1)
    ones = jnp.ones((V7X_LANES, V7X_LANES), BF16)
    tok0 = (jnp.arange(b, dtype=jnp.int32) * t_rows)[None, :, None]

    h = jnp.concatenate([x, ctx], axis=1).reshape(b * t_rows, d)

    for l in range(depth):
        m = mods[l]
        lat = lambda i, j: jnp.stack([m[:b, i], m[:b, j]], axis=1)
        cxm = lambda i, j: jnp.stack([m[b, i], m[b, j]], axis=0)

        wl = w_in[l]
        sec ={"cb": wl[:, o_cb:o_cc], "cc": wl[:, o_cc:o_cx], "cx": wl[:, o_cx:o_nq], "cq": wl[:, o_cq:o_ckv],
               "nq": wl[:, o_nq:o_nk], "nk": wl[:, o_nk:o_nv], "nv": wl[:, o_nv:o_nv + na_cols],
               "ckv": wl[:, o_ckv:o_kr], "kr": _spread_rope(wl[:, o_kr:o_cb])}
        w_p = jnp.zeros((d, lay["n"]), BF16)
        for name, blk in sec.items():
            w_p = lax.dynamic_update_slice(w_p, blk.astype(BF16), (0, lay[name]))
        wq = _pad_heads(mla_w_uq[l], mla_heads, nope).astype(BF16)
        wkv = mla_w_ukv[l].reshape(kv_rank, mla_heads, 2 * HEAD_DIM)
        wk = wkv[:, :, :HEAD_DIM].reshape(kv_rank, mla_cols).astype(BF16)
        wv = wkv[:, :, HEAD_DIM:].reshape(kv_rank, mla_cols).T.astype(BF16)
        gq = _pad_heads(mla_q_g[l][None, :], 1, nope)
        gk = _pad_heads(mla_k_g[l][None, :], 1, nope)
        wo = w_out[l].astype(BF16)
        wo_a, wo_b, wo_c = wo[:mla_cols], wo[mla_cols:mla_cols + conv_ch], wo[mla_cols + conv_ch:]
        wr = jnp.zeros((d, V7X_LANES), BF16).at[:, :n_exp].set(w_router[l].astype(BF16))

        px = _proj_in(h, norm1_g[l][None, :], lat(0, 1), cxm(0, 1), w_p, t_rows=t_rows, s_rows=s_rows)
        q, k, vt = _mla_prep(px, lay, mla_qa_g[l][None, :], mla_kva_g[l][None, :], wq, wk, wv, gq, gk, cos_t, sin_t,
                            t_rows=t_rows, heads=mla_heads, qk_dim=qk_dim, scale=qk_dim ** -0.5 * LOG2E)
        q3, k3 = q.reshape(b, t_rows, -1), k.reshape(b, t_rows, -1)
        o_a = jnp.concatenate([
            _flash(q3, k3, vt, heads=mla_heads, q_row0=0, n_q=s_rows, kv_row0=0, n_kv=t_rows,
                   tq=_pick(s_rows, 512, 256), tk=_pick(t_rows, 768, VT_TILE), name="mla_attn"),
            _flash(q3, k3, vt, heads=mla_heads, q_row0=s_rows, n_q=l_rows, kv_row0=s_rows, n_kv=l_rows,
                   tq=l_rows, tk=l_rows, name="mla_attn_ctx")], axis=1)
        o_b = _gated_conv(px, lay, conv_w[l], t_rows=t_rows, s_rows=s_rows)
        o_c = _na_attention(px.reshape(b, t_rows, -1), lay, na_q_g[l][None, :], na_k_g[l][None, :],
                            _na_table(na_rpb[l]), heads=na_heads, s_rows=s_rows, l_rows=l_rows, win_r=win_r,
                            scale=HEAD_DIM ** -0.5)
        h = _proj_out(o_a.reshape(b * t_rows, -1), o_b, o_c.reshape(b * t_rows, -1), wo_a, wo_b, wo_c, h,
                      m[:b, 2][:, None, :], m[b, 2][None, :], t_rows=t_rows, s_rows=s_rows)

        hn, aff = _moe_norm(h, norm2_g[l][None, :], lat(3, 4), cxm(3, 4), wr, b=b, t_rows=t_rows, s_rows=s_rows,
                            n_exp=n_exp)
        i_lat, g_lat = _topk(aff, tri, ones, row0=0, n=s_rows, k=cap_lat)
        i_ctx, g_ctx = _topk(aff, tri, ones, row0=s_rows, n=l_rows, k=cap_ctx)
        em = lambda a: jnp.transpose(a[:, :, :n_exp], (2, 0, 1))
        idx = jnp.concatenate([em(i_lat) + tok0, em(i_ctx) + tok0 + s_rows], axis=2)
        gates = jnp.concatenate([em(g_lat), em(g_ctx)], axis=2)
        h = _experts(idx[:, :, None, :], gates[..., None], m[:b, 5][:, None, :], m[b, 5][None, :],
                     ex_gate, ex_up, ex_down, l, hn, h, cap_lat=cap_lat)

    return h.reshape(b, t_rows, d)[:, :s_rows]
```

```python
import functools

import jax
import jax.numpy as jnp
import numpy as np
from jax import lax
from jax.experimental import pallas as pl
from jax.experimental.pallas import tpu as pltpu

F32 = jnp.float32
BF16 = jnp.bfloat16
HIGHEST = lax.Precision.HIGHEST

GRID_W = 64
HEAD_DIM = 128
MLA_ROPE = 64
ROPE_F = MLA_ROPE // 4
ROPE_BASE = 10000.0
CAP_FACTOR = 2
EPS = 1e-6
N_MOD = 6
LOG2E = 1.4426950408889634
NEG = -1e30

V7X_LANES = 128
V7X_MXU = 256
V7X_VMEM_LIMIT = 56 * 1024 * 1024
NA_QROWS = 4
NA_KROWS = 12
VT_TILE = 256
FLASH_PAIRS_PER_TRIP = 2


def _pick(n, target, mult):
    best = None
    for d in range(mult, min(n, target) + 1, mult):
        if n % d == 0:
            best = d
    assert best is not None, (n, target, mult)
    return best


def _cparams(sem):
    return pltpu.CompilerParams(dimension_semantics=sem, vmem_limit_bytes=V7X_VMEM_LIMIT)


def _nt_dot(a, b):
    return lax.dot_general(a, b, (((1,), (1,)), ((), ())), preferred_element_type=F32)


def _rms(x, eps):
    return x * lax.rsqrt(jnp.mean(x * x, axis=-1, keepdims=True) + eps)


def _ada_kernel(cond_ref, down_ref, up_ref, bias_ref, out_ref, t_ref):
    @pl.when(pl.program_id(1) == 0)
    def _():
        c = cond_ref[...]
        s = c / (1.0 + jnp.exp(-c))
        t_ref[...] = jnp.dot(s, down_ref[0], precision=HIGHEST, preferred_element_type=F32)

    out_ref[0] = jnp.dot(t_ref[...], up_ref[0], precision=HIGHEST, preferred_element_type=F32) + bias_ref[0]


def _ada_mod(cond, down, up, bias):
    depth, d, rk = down.shape
    n = up.shape[-1]
    tn = _pick(n, 4096, V7X_LANES)
    rows = cond.shape[0]
    return pl.pallas_call(
        _ada_kernel,
        grid=(depth, n // tn),
        in_specs=[
            pl.BlockSpec((rows, d), lambda l, j: (0, 0)),
            pl.BlockSpec((1, d, rk), lambda l, j: (l, 0, 0)),
            pl.BlockSpec((1, rk, tn), lambda l, j: (l, 0, j)),
            pl.BlockSpec((1, 1, tn), lambda l, j: (l, 0, j)),
        ],
        out_specs=pl.BlockSpec((1, rows, tn), lambda l, j: (l, 0, j)),
        out_shape=jax.ShapeDtypeStruct((depth, rows, n), F32),
        scratch_shapes=[pltpu.VMEM((rows, rk), F32)],
        compiler_params=_cparams(("arbitrary", "arbitrary")),
        name="ada_mod",
    )(cond, down, up, bias.reshape(depth, 1, n))


def _modulated_norm(x, g, lat_shift, lat_scale, ctx_shift, ctx_scale, is_ctx, eps):
    y = _rms(x, eps) * g
    shift = jnp.where(is_ctx, ctx_shift, lat_shift)
    scale = jnp.where(is_ctx, ctx_scale, lat_scale)
    return y * (1.0 + scale) + shift


def _row_is_ctx(tile_in_batch, tm, row0, rows, s_rows):
    r = tile_in_batch * tm + row0 + lax.broadcasted_iota(jnp.int32, (rows, 1), 0)
    return r >= s_rows


def _proj_in_kernel(h_ref, g_ref, lat_ref, ctx_ref, w_ref, o_ref, nx_ref, *, tm, tpb, s_rows, chunk, eps):
    @pl.when(pl.program_id(1) == 0)
    def _():
        tile = pl.program_id(0) % tpb

        def body(c, carry):
            r0 = pl.multiple_of(c * chunk, chunk)
            x = h_ref[pl.ds(r0, chunk), :]
            is_ctx = _row_is_ctx(tile, tm, r0, chunk, s_rows)
            y = _modulated_norm(x, g_ref[...], lat_ref[0, 0:1, :], lat_ref[0, 1:2, :],
                                ctx_ref[0:1, :], ctx_ref[1:2, :], is_ctx, eps)
            nx_ref[pl.ds(r0, chunk), :] = y.astype(BF16)
            return carry

        lax.fori_loop(0, tm // chunk, body, 0)

    o_ref[...] = jnp.dot(nx_ref[...], w_ref[...], preferred_element_type=F32).astype(o_ref.dtype)


def _proj_in(h, g, lat_mod, ctx_mod, w, *, t_rows, s_rows):
    r, d = h.shape
    n = w.shape[1]
    tm = _pick(t_rows, 768, 128)
    tn = _pick(n, 512, V7X_LANES)
    tpb = t_rows // tm
    kern = functools.partial(_proj_in_kernel, tm=tm, tpb=tpb, s_rows=s_rows, chunk=_pick(tm, 128, 8), eps=EPS)
    return pl.pallas_call(
        kern,
        grid=(r // tm, n // tn),
        in_specs=[
            pl.BlockSpec((tm, d), lambda i, j: (i, 0)),
            pl.BlockSpec((1, d), lambda i, j: (0, 0)),
            pl.BlockSpec((1, 2, d), lambda i, j: (i // tpb, 0, 0)),
            pl.BlockSpec((2, d), lambda i, j: (0, 0)),
            pl.BlockSpec((d, tn), lambda i, j: (0, j)),
        ],
        out_specs=pl.BlockSpec((tm, tn), lambda i, j: (i, j)),
        out_shape=jax.ShapeDtypeStruct((r, n), BF16),
        scratch_shapes=[pltpu.VMEM((tm, d), BF16)],
        compiler_params=_cparams(("parallel", "arbitrary")),
        name="proj_in",
    )(h, g, lat_mod, ctx_mod, w)


def _rope(r, cos, sin):
    return r * cos + pltpu.roll(r, V7X_LANES // 2, 1) * sin


def _mla_prep_kernel(cq_ref, ckv_ref, kr_ref, gqa_ref, gkva_ref, wq_ref, wk_ref, wv_ref, gq_ref, gk_ref,
                     cos_ref, sin_ref, q_ref, k_ref, vt_ref, *, heads, qk_dim, scale, eps):
    hd, hp = HEAD_DIM, V7X_MXU
    cqn = (_rms(cq_ref[...].astype(F32), eps) * gqa_ref[...]).astype(BF16)
    ckvn = (_rms(ckv_ref[...].astype(F32), eps) * gkva_ref[...]).astype(BF16)
    kr = kr_ref[...].astype(F32)
    cos, sin = cos_ref[...], sin_ref[...]
    kr2 = jnp.sum(kr * kr, axis=-1, keepdims=True)
    vt = _nt_dot(wv_ref[...], ckvn).astype(vt_ref.dtype)
    for c in range(vt_ref.shape[1]):
        vt_ref[0, c] = vt[:, c * VT_TILE:(c + 1) * VT_TILE]
    for h in range(heads):
        a = jnp.dot(cqn, wq_ref[:, h * hp:(h + 1) * hp], preferred_element_type=F32)
        inv = lax.rsqrt(jnp.sum(a * a, axis=-1, keepdims=True) / qk_dim + eps)
        an = a * inv * gq_ref[...]
        q_ref[:, h * hp:h * hp + hd] = (an[:, :hd] * scale).astype(q_ref.dtype)
        q_ref[:, h * hp + hd:(h + 1) * hp] = (_rope(an[:, hd:], cos, sin) * scale).astype(q_ref.dtype)
        kn = jnp.dot(ckvn, wk_ref[:, h * hd:(h + 1) * hd], preferred_element_type=F32)
        invk = lax.rsqrt((jnp.sum(kn * kn, axis=-1, keepdims=True) + kr2) / qk_dim + eps)
        k_ref[:, h * hp:h * hp + hd] = (kn * invk * gk_ref[:, :hd]).astype(k_ref.dtype)
        k_ref[:, h * hp + hd:(h + 1) * hp] = _rope(kr * invk * gk_ref[:, hd:], cos, sin).astype(k_ref.dtype)


def _mla_prep(px, lay, gqa, gkva, wq, wk, wv, gq, gk, cos, sin, *, t_rows, heads, qk_dim, scale):
    r = px.shape[0]
    tm = VT_TILE
    tpb = t_rows // tm
    vpt = tm // VT_TILE
    qr, kvr = gqa.shape[1], gkva.shape[1]
    kern = functools.partial(_mla_prep_kernel, heads=heads, qk_dim=qk_dim, scale=scale, eps=EPS)
    full = lambda a: pl.BlockSpec(a.shape, lambda i: (0,) * a.ndim, pipeline_mode=pl.Buffered(1))
    return pl.pallas_call(
        kern,
        grid=(r // tm,),
        in_specs=[
            pl.BlockSpec((tm, qr), lambda i: (i, lay["cq"] // qr)),
            pl.BlockSpec((tm, kvr), lambda i: (i, lay["ckv"] // kvr)),
            pl.BlockSpec((tm, V7X_LANES), lambda i: (i, lay["kr"] // V7X_LANES)),
            full(gqa), full(gkva), full(wq), full(wk), full(wv), full(gq), full(gk),
            pl.BlockSpec((tm, V7X_LANES), lambda i: (i % tpb, 0)),
            pl.BlockSpec((tm, V7X_LANES), lambda i: (i % tpb, 0)),
        ],
        out_specs=[
            pl.BlockSpec((tm, heads * V7X_MXU), lambda i: (i, 0)),
            pl.BlockSpec((tm, heads * V7X_MXU), lambda i: (i, 0)),
            pl.BlockSpec((1, vpt, heads * HEAD_DIM, VT_TILE), lambda i: (i // tpb, i % tpb, 0, 0)),
        ],
        out_shape=[
            jax.ShapeDtypeStruct((r, heads * V7X_MXU), BF16),
            jax.ShapeDtypeStruct((r, heads * V7X_MXU), BF16),
            jax.ShapeDtypeStruct((r // t_rows, t_rows // VT_TILE, heads * HEAD_DIM, VT_TILE), BF16),
        ],
        compiler_params=_cparams(("parallel",)),
        name="mla_prep",
    )(px, px, px, gqa, gkva, wq, wk, wv, gq, gk, cos, sin)


def _flash_kernel(q_ref, k_ref, vt_ref, o_ref, s0_ref, s1_ref, mx0_ref, mx1_ref, m_ref, l_ref, acc_ref, *, tk, nkv):
    m_ref[...] = jnp.full(m_ref.shape, -jnp.inf, F32)
    l_ref[...] = jnp.zeros(l_ref.shape, F32)
    acc_ref[...] = jnp.zeros(acc_ref.shape, F32)
    vpt = tk // VT_TILE

    def scores(j, s_ref, mx_ref):
        off = pl.multiple_of(j * tk, tk)
        st = _nt_dot(k_ref[0, pl.ds(off, tk), :], q_ref[0])
        s_ref[...] = st
        mx_ref[...] = jnp.max(st, axis=0, keepdims=True)

    def accumulate(j, s_ref, mx_ref):
        m_prev = m_ref[...]
        m_new = jnp.maximum(m_prev, mx_ref[...])
        alpha = jnp.exp2(m_prev - m_new)
        p = jnp.exp2(s_ref[...] - m_new)
        l_ref[...] = alpha * l_ref[...] + jnp.sum(p, axis=0, keepdims=True)
        pb = p.astype(BF16)
        pv = jnp.dot(vt_ref[0, j * vpt], pb[0:VT_TILE], preferred_element_type=F32)
        for c in range(1, vpt):
            pv = pv + jnp.dot(vt_ref[0, j * vpt + c], pb[c * VT_TILE:(c + 1) * VT_TILE],
                              preferred_element_type=F32)
        acc_ref[...] = alpha * acc_ref[...] + pv
        m_ref[...] = m_new

    scores(0, s0_ref, mx0_ref)
    npairs = (nkv - 1) // 2

    def pair(j):
        scores(j + 1, s1_ref, mx1_ref)
        accumulate(j, s0_ref, mx0_ref)
        scores(j + 2, s0_ref, mx0_ref)
        accumulate(j + 1, s1_ref, mx1_ref)

    def body(jj, carry):
        for u in range(FLASH_PAIRS_PER_TRIP):
            pair(2 * (jj * FLASH_PAIRS_PER_TRIP + u))
        return carry

    ntrips = npairs // FLASH_PAIRS_PER_TRIP
    lax.fori_loop(0, ntrips, body, 0)
    for pr in range(ntrips * FLASH_PAIRS_PER_TRIP, npairs):
        pair(2 * pr)
    if (nkv - 1) % 2 == 0:
        accumulate(nkv - 1, s0_ref, mx0_ref)
    else:
        scores(nkv - 1, s1_ref, mx1_ref)
        accumulate(nkv - 2, s0_ref, mx0_ref)
        accumulate(nkv - 1, s1_ref, mx1_ref)
    o_ref[0] = (acc_ref[...] / l_ref[...]).T.astype(o_ref.dtype)


def _flash(q, k, vt, *, heads, q_row0, n_q, kv_row0, n_kv, tq, tk, name):
    b = q.shape[0]
    dk, dv = V7X_MXU, HEAD_DIM
    assert q_row0 % tq == 0 and n_q % tq == 0 and kv_row0 % n_kv == 0 and n_kv % tk == 0
    assert tk % VT_TILE == 0
    kern = functools.partial(_flash_kernel, tk=tk, nkv=n_kv // tk)
    return pl.pallas_call(
        kern,
        grid=(b, heads, n_q // tq),
        in_specs=[
            pl.BlockSpec((1, tq, dk), lambda bi, h, i: (bi, q_row0 // tq + i, h)),
            pl.BlockSpec((1, n_kv, dk), lambda bi, h, i: (bi, kv_row0 // n_kv, h)),
            pl.BlockSpec((1, n_kv // VT_TILE, dv, VT_TILE), lambda bi, h, i: (bi, kv_row0 // n_kv, h, 0)),
        ],
        out_specs=pl.BlockSpec((1, tq, dv), lambda bi, h, i: (bi, i, h)),
        out_shape=jax.ShapeDtypeStruct((b, n_q, heads * dv), BF16),
        scratch_shapes=[pltpu.VMEM((tk, tq), F32), pltpu.VMEM((tk, tq), F32),
                        pltpu.VMEM((1, tq), F32), pltpu.VMEM((1, tq), F32),
                        pltpu.VMEM((1, tq), F32), pltpu.VMEM((1, tq), F32), pltpu.VMEM((dv, tq), F32)],
        compiler_params=_cparams(("parallel", "parallel", "arbitrary")),
        name=name,
    )(q, k, vt)


def _na_table_kernel(rpb_ref, tbl_ref, *, win_r, win_c, w):
    h = pl.program_id(0)
    q = lax.broadcasted_iota(jnp.int32, (w, 2 * w), 0)
    lane = lax.broadcasted_iota(jnp.int32, (w, 2 * w), 1)
    c0 = jnp.clip(q - win_c // 2, 0, w - win_c)
    n_d = 2 * win_r - 1
    for side in range(2):
        kc = lane - side * w
        inside = (kc >= c0) & (kc < c0 + win_c)
        own_half = (kc >= 0) & (kc < w)
        rel = kc - q + (win_c - 1)
        for d in range(n_d):
            acc = jnp.full((w, 2 * w), NEG, F32)
            for dc in range(2 * win_c - 1):
                acc = jnp.where(inside & (rel == dc), rpb_ref[h, d, dc] * LOG2E, acc)
            tbl_ref[0, side, d] = jnp.where(own_half, acc, 0.0)
        tbl_ref[0, side, n_d] = jnp.where(own_half, NEG, 0.0)


def _na_table(rpb):
    heads, n_d, n_c = rpb.shape
    win_r, win_c = (n_d + 1) // 2, (n_c + 1) // 2
    kern = functools.partial(_na_table_kernel, win_r=win_r, win_c=win_c, w=GRID_W)
    return pl.pallas_call(
        kern,
        grid=(heads,),
        in_specs=[pl.BlockSpec(memory_space=pltpu.SMEM)],
        out_specs=pl.BlockSpec((1, 2, n_d + 1, GRID_W, 2 * GRID_W), lambda h: (h, 0, 0, 0, 0)),
        out_shape=jax.ShapeDtypeStruct((heads, 2, n_d + 1, GRID_W, 2 * GRID_W), F32),
        compiler_params=_cparams(("arbitrary",)),
        name="na_table",
    )(rpb)


def _na_kernel(q_ref, k_ref, v_ref, gq_ref, gk_ref, tbl_ref, o_ref, kn_ref, bias_ref, sl0_ref, sc0_ref,
               sl1_ref, sc1_ref, *, s_rows, l_rows, grid_rows, win_r, scale, eps, chunk):
    w = GRID_W
    t_rows = s_rows + l_rows
    nq, nk = NA_QROWS * w, NA_KROWS * w

    def kn_body(c, carry):
        r0 = pl.multiple_of(c * chunk, chunk)
        kk = k_ref[0, pl.ds(r0, chunk), :].astype(F32)
        kn_ref[pl.ds(r0, chunk), :] = (_rms(kk, eps) * gk_ref[...]).astype(BF16)
        return carry

    lax.fori_loop(0, t_rows // chunk, kn_body, 0)
    kc = kn_ref[s_rows:t_rows, :]
    vc = v_ref[0, s_rows:t_rows, :]

    def norm_q(x):
        return (_rms(x.astype(F32), eps) * gq_ref[...] * scale).astype(BF16)

    def softmax_pv(parts):
        m = functools.reduce(jnp.maximum, [jnp.max(s, axis=-1, keepdims=True) for s, _ in parts])
        ps = [jnp.exp2(s - m) for s, _ in parts]
        den = functools.reduce(lambda a, b: a + b, [jnp.sum(p, axis=-1, keepdims=True) for p in ps])
        num = functools.reduce(lambda a, b: a + b,
                               [jnp.dot(p.astype(BF16), vv, preferred_element_type=F32)
                                for p, (_, vv) in zip(ps, parts)])
        return num / den

    def key_start(i):
        return jnp.clip(i * NA_QROWS - win_r // 2, 0, grid_rows - NA_KROWS)

    def scores(i, sl_ref, sc_ref):
        q0 = pl.multiple_of(i * nq, nq)
        ks = key_start(i)
        k0 = pl.multiple_of(ks * w, w)
        qn = norm_q(q_ref[0, pl.ds(q0, nq), :])
        for qr in range(NA_QROWS):
            r = i * NA_QROWS + qr
            r0 = jnp.clip(r - win_r // 2, 0, grid_rows - win_r)
            for p in range(NA_KROWS // 2):
                ds = []
                for side in range(2):
                    kr = ks + 2 * p + side
                    ok = (kr >= r0) & (kr < r0 + win_r)
                    ds.append(jnp.where(ok, kr - r + (win_r - 1), 2 * win_r - 1))
                bias_ref[qr * w:(qr + 1) * w, p * 2 * w:(p + 1) * 2 * w] = (
                    tbl_ref[0, 0, ds[0]] + tbl_ref[0, 1, ds[1]])
        sl_ref[...] = _nt_dot(qn, kn_ref[pl.ds(k0, nk), :]) + bias_ref[...]
        sc_ref[...] = _nt_dot(qn, kc)

    def attend(i, sl_ref, sc_ref):
        q0 = pl.multiple_of(i * nq, nq)
        k0 = pl.multiple_of(key_start(i) * w, w)
        o = softmax_pv([(sc_ref[...], vc), (sl_ref[...], v_ref[0, pl.ds(k0, nk), :])])
        o_ref[0, pl.ds(q0, nq), :] = o.astype(o_ref.dtype)

    nblk = grid_rows // NA_QROWS
    scores(0, sl0_ref, sc0_ref)
    npairs = (nblk - 1) // 2

    def pair_body(jj, carry):
        i = 2 * jj
        scores(i + 1, sl1_ref, sc1_ref)
        attend(i, sl0_ref, sc0_ref)
        scores(i + 2, sl0_ref, sc0_ref)
        attend(i + 1, sl1_ref, sc1_ref)
        return carry

    lax.fori_loop(0, npairs, pair_body, 0)
    if (nblk - 1) % 2 == 0:
        attend(nblk - 1, sl0_ref, sc0_ref)
    else:
        scores(nblk - 1, sl1_ref, sc1_ref)
        attend(nblk - 2, sl0_ref, sc0_ref)
        attend(nblk - 1, sl1_ref, sc1_ref)

    qc = norm_q(q_ref[0, s_rows:t_rows, :])
    o_ref[0, s_rows:t_rows, :] = softmax_pv([(_nt_dot(qc, kc), vc)]).astype(o_ref.dtype)


def _na_attention(px3, lay, gq, gk, tbl, *, heads, s_rows, l_rows, win_r, scale):
    b, t, _ = px3.shape
    hd = HEAD_DIM
    grid_rows = s_rows // GRID_W
    assert 2 * GRID_W == V7X_LANES and grid_rows % NA_QROWS == 0 and grid_rows >= NA_KROWS
    assert NA_KROWS >= NA_QROWS + win_r - 1 and NA_KROWS % 2 == 0
    kern = functools.partial(_na_kernel, s_rows=s_rows, l_rows=l_rows, grid_rows=grid_rows, win_r=win_r,
                             scale=scale, eps=EPS, chunk=_pick(t, 256, 16))
    col = lambda name: (lambda bi, h: (bi, 0, lay[name] // hd + h))
    return pl.pallas_call(
        kern,
        grid=(b, heads),
        in_specs=[
            pl.BlockSpec((1, t, hd), col("nq")),
            pl.BlockSpec((1, t, hd), col("nk")),
            pl.BlockSpec((1, t, hd), col("nv")),
            pl.BlockSpec((1, hd), lambda bi, h: (0, 0)),
            pl.BlockSpec((1, hd), lambda bi, h: (0, 0)),
            pl.BlockSpec((1,) + tbl.shape[1:], lambda bi, h: (h, 0, 0, 0, 0)),
        ],
        out_specs=pl.BlockSpec((1, t, hd), lambda bi, h: (bi, 0, h)),
        out_shape=jax.ShapeDtypeStruct((b, t, heads * hd), BF16),
        scratch_shapes=[pltpu.VMEM((t, hd), BF16), pltpu.VMEM((NA_QROWS * GRID_W, NA_KROWS * GRID_W), F32),
                        pltpu.VMEM((NA_QROWS * GRID_W, NA_KROWS * GRID_W), F32),
                        pltpu.VMEM((NA_QROWS * GRID_W, l_rows), F32),
                        pltpu.VMEM((NA_QROWS * GRID_W, NA_KROWS * GRID_W), F32),
                        pltpu.VMEM((NA_QROWS * GRID_W, l_rows), F32)],
        compiler_params=_cparams(("parallel", "parallel")),
        name="na_attn",
    )(px3, px3, px3, gq, gk, tbl)


def _conv_kernel(cb_ref, cc_ref, cx_ref, ccp_ref, cxp_ref, ccn_ref, cxn_ref, w_ref, o_ref, *,
                 tm, tpb, s_rows, t_rows, halo):
    u = cc_ref[...].astype(F32) * cx_ref[...].astype(F32)
    u_before = ccp_ref[halo - 1:halo, :].astype(F32) * cxp_ref[halo - 1:halo, :].astype(F32)
    u_after = ccn_ref[0:1, :].astype(F32) * cxn_ref[0:1, :].astype(F32)
    loc = lax.broadcasted_iota(jnp.int32, (tm, 1), 0)
    row = (pl.program_id(0) % tpb) * tm + loc
    first = (row == 0) | (row == s_rows)
    last = (row == s_rows - 1) | (row == t_rows - 1)
    u_prev = jnp.where(loc == 0, u_before, pltpu.roll(u, 1, 0))
    u_prev = jnp.where(first, 0.0, u_prev)
    u_next = jnp.where(loc == tm - 1, u_after, pltpu.roll(u, tm - 1, 0))
    u_next = jnp.where(last, 0.0, u_next)
    y = w_ref[0:1, :] * u_prev + w_ref[1:2, :] * u + w_ref[2:3, :] * u_next
    o_ref[...] = (cb_ref[...].astype(F32) * y).astype(o_ref.dtype)


def _gated_conv(px, lay, w, *, t_rows, s_rows):
    r = px.shape[0]
    c = w.shape[1]
    halo = 16
    tm = _pick(t_rows, 256, 128)
    tpb = t_rows // tm
    nb = r // halo
    kern = functools.partial(_conv_kernel, tm=tm, tpb=tpb, s_rows=s_rows, t_rows=t_rows, halo=halo)
    blk = lambda name: pl.BlockSpec((tm, c), lambda i: (i, lay[name] // c))
    prev = lambda name: pl.BlockSpec((halo, c), lambda i: (jnp.maximum(i * (tm // halo) - 1, 0), lay[name] // c))
    nxt = lambda name: pl.BlockSpec((halo, c), lambda i: (jnp.minimum((i + 1) * (tm // halo), nb - 1), lay[name] // c))
    return pl.pallas_call(
        kern,
        grid=(r // tm,),
        in_specs=[blk("cb"), blk("cc"), blk("cx"), prev("cc"), prev("cx"), nxt("cc"), nxt("cx"),
                  pl.BlockSpec(w.shape, lambda i: (0, 0))],
        out_specs=pl.BlockSpec((tm, c), lambda i: (i, 0)),
        out_shape=jax.ShapeDtypeStruct((r, c), BF16),
        compiler_params=_cparams(("parallel",)),
        name="gated_conv",
    )(px, px, px, px, px, px, px, w)


def _proj_out_kernel(oa_ref, ob_ref, oc_ref, wa_ref, wb_ref, wc_ref, h_ref, lat_ref, ctx_ref, o_ref, *,
                     tm, tpb, s_rows):
    y = jnp.dot(oa_ref[...], wa_ref[...], preferred_element_type=F32)
    y = y + jnp.dot(ob_ref[...], wb_ref[...], preferred_element_type=F32)
    y = y + jnp.dot(oc_ref[...], wc_ref[...], preferred_element_type=F32)
    is_ctx = _row_is_ctx(pl.program_id(0) % tpb, tm, 0, tm, s_rows)
    gate = jnp.where(is_ctx, ctx_ref[...], lat_ref[0])
    o_ref[...] = h_ref[...] + gate * y


def _proj_out(oa, ob, oc, wa, wb, wc, h, lat_gate, ctx_gate, *, t_rows, s_rows):
    r, d = h.shape
    tm = _pick(t_rows, 768, 128)
    tn = _pick(d, 1024, V7X_LANES)
    tpb = t_rows // tm
    kern = functools.partial(_proj_out_kernel, tm=tm, tpb=tpb, s_rows=s_rows)
    lhs = lambda a: pl.BlockSpec((tm, a.shape[1]), lambda i, j: (i, 0))
    rhs = lambda a: pl.BlockSpec((a.shape[0], tn), lambda i, j: (0, j))
    return pl.pallas_call(
        kern,
        grid=(r // tm, d // tn),
        in_specs=[lhs(oa), lhs(ob), lhs(oc), rhs(wa), rhs(wb), rhs(wc),
                  pl.BlockSpec((tm, tn), lambda i, j: (i, j)),
                  pl.BlockSpec((1, 1, tn), lambda i, j: (i // tpb, 0, j)),
                  pl.BlockSpec((1, tn), lambda i, j: (0, j))],
        out_specs=pl.BlockSpec((tm, tn), lambda i, j: (i, j)),
        out_shape=jax.ShapeDtypeStruct((r, d), F32),
        input_output_aliases={6: 0},
        compiler_params=_cparams(("parallel", "arbitrary")),
        name="proj_out",
    )(oa, ob, oc, wa, wb, wc, h, lat_gate, ctx_gate)


def _moe_norm_kernel(h_ref, g_ref, lat_ref, ctx_ref, wr_ref, hn_ref, aff_ref, *, tm, tpb, s_rows, n_exp, eps):
    is_ctx = _row_is_ctx(pl.program_id(0) % tpb, tm, 0, tm, s_rows)
    hn = _modulated_norm(h_ref[...], g_ref[...], lat_ref[0, 0:1, :], lat_ref[0, 1:2, :],
                         ctx_ref[0:1, :], ctx_ref[1:2, :], is_ctx, eps)
    hb = hn.astype(BF16)
    half = hn.shape[1] // 2
    hf = hb.astype(F32)
    hn_ref[...] = pltpu.bitcast(hf[:, :half], jnp.uint32) | (pltpu.bitcast(hf[:, half:], jnp.uint32) >> 16)
    logits = jnp.dot(hb, wr_ref[...], preferred_element_type=F32)
    lane = lax.broadcasted_iota(jnp.int32, logits.shape, 1)
    logits = jnp.where(lane < n_exp, logits, -jnp.inf)
    e = jnp.exp(logits - jnp.max(logits, axis=-1, keepdims=True))
    aff = e / jnp.sum(e, axis=-1, keepdims=True)
    aff_ref[0] = aff.T[:n_exp, :]


def _moe_norm(h, g, lat_mod, ctx_mod, wr, *, b, t_rows, s_rows, n_exp):
    r, d = h.shape
    tm = _pick(t_rows, 256, 128)
    tpb = t_rows // tm
    kern = functools.partial(_moe_norm_kernel, tm=tm, tpb=tpb, s_rows=s_rows, n_exp=n_exp, eps=EPS)
    return pl.pallas_call(
        kern,
        grid=(r // tm,),
        in_specs=[
            pl.BlockSpec((tm, d), lambda i: (i, 0)),
            pl.BlockSpec((1, d), lambda i: (0, 0)),
            pl.BlockSpec((1, 2, d), lambda i: (i // tpb, 0, 0)),
            pl.BlockSpec((2, d), lambda i: (0, 0)),
            pl.BlockSpec(wr.shape, lambda i: (0, 0)),
        ],
        out_specs=[pl.BlockSpec((tm, d // 2), lambda i: (i, 0)),
                   pl.BlockSpec((1, n_exp, tm), lambda i: (i // tpb, 0, i % tpb))],
        out_shape=[jax.ShapeDtypeStruct((r, d // 2), jnp.uint32), jax.ShapeDtypeStruct((b, n_exp, t_rows), F32)],
        compiler_params=_cparams(("parallel",)),
        name="moe_norm",
    )(h, g, lat_mod, ctx_mod, wr)


def _topk_kernel(aff_ref, tri_ref, ones_ref, idx_ref, gate_ref, posm_ref, *, n, k, n_exp):
    ln = V7X_LANES
    nt = n // ln
    a = aff_ref[0]
    bits = pltpu.bitcast(a, jnp.int32)
    thr = jnp.zeros((n_exp, 1), jnp.int32)
    for bit in range(30, -1, -1):
        cand = thr | (1 << bit)
        cnt = jnp.sum((bits >= cand).astype(jnp.int32), axis=1, keepdims=True)
        thr = jnp.where(cnt >= k, cand, thr)
    gt = bits > thr
    eq = bits == thr

    def excl_prefix(x):
        xb = x.astype(BF16)
        run = jnp.zeros((n_exp, ln), F32)
        outs = []
        for t in range(nt):
            blk = xb[:, t * ln:(t + 1) * ln]
            outs.append(jnp.dot(blk, tri_ref[...], preferred_element_type=F32) + run)
            run = run + jnp.dot(blk, ones_ref[...], preferred_element_type=F32)
        return jnp.concatenate(outs, axis=1)

    need = (k - jnp.sum(gt.astype(jnp.int32), axis=1, keepdims=True)).astype(F32)
    eqf = jnp.where(eq, 1.0, 0.0)
    sel = gt | (eq & (excl_prefix(eqf) < need))
    pos = excl_prefix(jnp.where(sel, 1.0, 0.0))
    posm_ref[...] = jnp.where(sel, pos, -1.0)

    lane = lax.broadcasted_iota(jnp.int32, (8, ln), 1)
    lane_f = lane.astype(F32)

    def jb_body(jb, carry):
        slot = (jb * 8 + lax.broadcasted_iota(jnp.int32, (8, 1), 0)).astype(F32)
        out_i = jnp.zeros((8, ln), F32)
        out_g = jnp.zeros((8, ln), F32)
        for e in range(n_exp):
            acc_i = jnp.zeros((8, ln), F32)
            acc_g = jnp.zeros((8, ln), F32)
            for t in range(nt):
                hit = posm_ref[e:e + 1, t * ln:(t + 1) * ln] == slot
                acc_i = acc_i + jnp.where(hit, lane_f + float(t * ln), 0.0)
                acc_g = acc_g + jnp.where(hit, aff_ref[0, e:e + 1, t * ln:(t + 1) * ln], 0.0)
            out_i = jnp.where(lane == e, jnp.sum(acc_i, axis=1, keepdims=True), out_i)
            out_g = jnp.where(lane == e, jnp.sum(acc_g, axis=1, keepdims=True), out_g)
        r0 = pl.multiple_of(jb * 8, 8)
        idx_ref[0, pl.ds(r0, 8), :] = out_i.astype(jnp.int32)
        gate_ref[0, pl.ds(r0, 8), :] = out_g
        return carry

    lax.fori_loop(0, k // 8, jb_body, 0)


def _topk(aff, tri, ones, *, row0, n, k):
    b, n_exp, _ = aff.shape
    assert row0 % n == 0 and n % V7X_LANES == 0 and k % 8 == 0 and n_exp <= V7X_LANES
    kern = functools.partial(_topk_kernel, n=n, k=k, n_exp=n_exp)
    return pl.pallas_call(
        kern,
        grid=(b,),
        in_specs=[pl.BlockSpec((1, n_exp, n), lambda bi: (bi, 0, row0 // n)),
                  pl.BlockSpec(tri.shape, lambda bi: (0, 0)),
                  pl.BlockSpec(ones.shape, lambda bi: (0, 0))],
        out_specs=[pl.BlockSpec((1, k, V7X_LANES), lambda bi: (bi, 0, 0)),
                   pl.BlockSpec((1, k, V7X_LANES), lambda bi: (bi, 0, 0))],
        out_shape=[jax.ShapeDtypeStruct((b, k, V7X_LANES), jnp.int32),
                   jax.ShapeDtypeStruct((b, k, V7X_LANES), F32)],
        scratch_shapes=[pltpu.VMEM((n_exp, n), F32)],
        compiler_params=_cparams(("parallel",)),
        name="topk",
    )(aff, tri, ones)


ROW_UNROLL = 8
SEM_HN, SEM_H, SEM_OUT = 0, 2, 3


def _expert_kernel(idx_ref, g_ref, lat_ref, ctx_ref, wg_ref, wu_ref, wd_ref, hn_hbm, h_hbm, out_hbm,
                   xe_ref, y_ref, st_ref, sem, *, rows, cap_lat, dcols, nchunk):
    del h_hbm
    e, bi, f = pl.program_id(0), pl.program_id(1), pl.program_id(2)
    ne, nb, nf = pl.num_programs(0), pl.num_programs(1), pl.num_programs(2)
    ch = rows // nchunk
    half = xe_ref.shape[1] // 2

    def issue_rows(n_rows, make):
        def trip(jj, carry):
            base = pl.multiple_of(jj * ROW_UNROLL, ROW_UNROLL)
            for u in range(ROW_UNROLL):
                make(base, u).start()
            return carry

        lax.fori_loop(0, n_rows // ROW_UNROLL, trip, 0)

    def hn_row(c, base, u):
        return pltpu.make_async_copy(hn_hbm.at[pl.ds(idx_ref[0, 0, 0, c * ch + base + u], 1)],
                                     st_ref.at[c % 2, pl.ds(base + u, 1)], sem.at[SEM_HN + c % 2])

    def hn_chunk_all(c):
        return pltpu.make_async_copy(hn_hbm.at[pl.ds(0, ch)], st_ref.at[c % 2], sem.at[SEM_HN + c % 2])

    def h_row(base, u):
        return pltpu.make_async_copy(out_hbm.at[pl.ds(idx_ref[0, 0, 0, base + u], 1)],
                                     y_ref.at[pl.ds(base + u, 1)], sem.at[SEM_H])

    def out_row(base, u):
        return pltpu.make_async_copy(y_ref.at[pl.ds(base + u, 1)],
                                     out_hbm.at[pl.ds(idx_ref[0, 0, 0, base + u], 1)], sem.at[SEM_OUT])

    def all_rows(s):
        return pltpu.make_async_copy(out_hbm.at[pl.ds(0, rows)], y_ref, sem.at[s])

    @pl.when(f == 0)
    def _():
        issue_rows(ch, functools.partial(hn_row, 0))

        @pl.when((e > 0) | (bi > 0))
        def _():
            all_rows(SEM_OUT).wait()

        issue_rows(rows, h_row)
        for c in range(nchunk):
            if c + 1 < nchunk:
                issue_rows(ch, functools.partial(hn_row, c + 1))
            hn_chunk_all(c).wait()
            p = st_ref[c % 2]
            hi = pltpu.bitcast(p & jnp.uint32(0xFFFF0000), F32)
            lo = pltpu.bitcast(p << 16, F32)
            xe_ref[c * ch:(c + 1) * ch, :half] = hi.astype(BF16)
            xe_ref[c * ch:(c + 1) * ch, half:] = lo.astype(BF16)

    xe = xe_ref[...]
    a = jnp.dot(xe, wg_ref[0, 0].astype(BF16), preferred_element_type=F32)
    u = jnp.dot(xe, wu_ref[0, 0].astype(BF16), preferred_element_type=F32)
    hm = ((a / (1.0 + jnp.exp(-a))) * u).astype(BF16)

    @pl.when(f == 0)
    def _():
        all_rows(SEM_H).wait()

    g = g_ref[0, 0]
    d = y_ref.shape[1]
    for c in range(d // dcols):
        cs = slice(c * dcols, (c + 1) * dcols)
        y = jnp.dot(hm, wd_ref[0, 0, :, cs].astype(BF16), preferred_element_type=F32) * g
        y_ref[0:cap_lat, cs] = y_ref[0:cap_lat, cs] + lat_ref[0, :, cs] * y[0:cap_lat]
        y_ref[cap_lat:rows, cs] = y_ref[cap_lat:rows, cs] + ctx_ref[:, cs] * y[cap_lat:rows]

    @pl.when(f == nf - 1)
    def _():
        issue_rows(rows, out_row)

        @pl.when((e == ne - 1) & (bi == nb - 1))
        def _():
            all_rows(SEM_OUT).wait()


def _experts(idx, gates, lat_gate, ctx_gate, wg, wu, wd, layer, hn, h, *, cap_lat):
    n_exp, b, _, rows = idx.shape
    d, ff = wg.shape[2], wg.shape[3]
    fc = _pick(ff, 256, V7X_LANES)
    nchunk = max(n for n in range(1, 7) if rows % (16 * n) == 0)
    assert rows % 16 == 0 and hn.shape[1] * 2 == d
    kern = functools.partial(_expert_kernel, rows=rows, cap_lat=cap_lat, dcols=_pick(d, 1024, V7X_LANES),
                             nchunk=nchunk)
    return pl.pallas_call(
        kern,
        grid=(n_exp, b, ff // fc),
        in_specs=[
            pl.BlockSpec((1, 1, 1, rows), lambda e, bi, f: (e, bi, 0, 0), memory_space=pltpu.SMEM),
            pl.BlockSpec((1, 1, rows, 1), lambda e, bi, f: (e, bi, 0, 0)),
            pl.BlockSpec((1, 1, d), lambda e, bi, f: (bi, 0, 0)),
            pl.BlockSpec((1, d), lambda e, bi, f: (0, 0)),
            pl.BlockSpec((1, 1, d, fc), lambda e, bi, f: (layer, e, 0, f)),
            pl.BlockSpec((1, 1, d, fc), lambda e, bi, f: (layer, e, 0, f)),
            pl.BlockSpec((1, 1, fc, d), lambda e, bi, f: (layer, e, f, 0)),
            pl.BlockSpec(memory_space=pl.ANY),
            pl.BlockSpec(memory_space=pl.ANY),
        ],
        out_specs=pl.BlockSpec(memory_space=pl.ANY),
        out_shape=jax.ShapeDtypeStruct(h.shape, F32),
        scratch_shapes=[pltpu.VMEM((rows, d), BF16), pltpu.VMEM((rows, d), F32),
                        pltpu.VMEM((2, rows // nchunk, d // 2), jnp.uint32), pltpu.SemaphoreType.DMA((4,))],
        input_output_aliases={8: 0},
        compiler_params=_cparams(("arbitrary", "arbitrary", "arbitrary")),
        name="experts",
    )(idx, gates, lat_gate, ctx_gate, wg, wu, wd, hn, h)


def _in_layout(conv_ch, q_rank, kv_rank, na_cols):
    lay, off = {}, 0

    def put(name, width, align):
        nonlocal off
        off = -(-off // align) * align
        lay[name] = off
        off += width

    for name in ("cb", "cc", "cx"):
        put(name, conv_ch, conv_ch)
    put("cq", q_rank, q_rank)
    for name in ("nq", "nk", "nv"):
        put(name, na_cols, HEAD_DIM)
    put("ckv", kv_rank, kv_rank)
    put("kr", V7X_LANES, V7X_LANES)
    lay["n"] = -(-off // 512) * 512
    return lay


def _spread_rope(a):
    half = MLA_ROPE // 2
    z = jnp.zeros(a.shape[:-1] + (V7X_LANES // 2 - half,), a.dtype)
    return jnp.concatenate([a[..., :half], z, a[..., half:], z], axis=-1)


def _pad_heads(a, heads, nope):
    a = a.reshape(a.shape[:-1] + (heads, nope + MLA_ROPE))
    a = jnp.concatenate([a[..., :nope], _spread_rope(a[..., nope:])], axis=-1)
    return a.reshape(a.shape[:-2] + (heads * V7X_MXU,))


def _rope_tables(s_rows, l_rows):
    pos = jnp.arange(s_rows, dtype=jnp.int32)
    row = (pos // GRID_W).astype(F32)
    col = (pos % GRID_W).astype(F32)
    inv = jnp.power(ROPE_BASE, -jnp.arange(ROPE_F, dtype=F32) / ROPE_F)
    ang = jnp.concatenate([row[:, None] * inv, col[:, None] * inv], axis=1)
    cos = jnp.concatenate([jnp.cos(ang), jnp.ones((l_rows, 2 * ROPE_F), F32)], axis=0)
    sin = jnp.concatenate([jnp.sin(ang), jnp.zeros((l_rows, 2 * ROPE_F), F32)], axis=0)
    return _spread_rope(jnp.concatenate([cos, cos], axis=1)), _spread_rope(jnp.concatenate([-sin, sin], axis=1))


def kernel(x, c, ctx, c_ctx, ada_down, ada_up, ada_bias, norm1_g, w_in, mla_qa_g, mla_kva_g, mla_w_uq, mla_w_ukv, mla_q_g, mla_k_g, conv_w, na_q_g, na_k_g, na_rpb, w_out, norm2_g, w_router, ex_gate, ex_up, ex_down):
    b, s_rows, d = x.shape
    l_rows = ctx.shape[1]
    t_rows = s_rows + l_rows
    depth = w_in.shape[0]
    q_rank, kv_rank = mla_qa_g.shape[1], mla_kva_g.shape[1]
    qk_dim = mla_q_g.shape[1]
    nope = qk_dim - MLA_ROPE
    assert nope == HEAD_DIM
    mla_heads = mla_w_uq.shape[2] // qk_dim
    conv_ch = conv_w.shape[2]
    na_heads = na_rpb.shape[1]
    na_cols = na_heads * HEAD_DIM
    win_r = (na_rpb.shape[2] + 1) // 2
    n_exp = w_router.shape[2]
    cap_lat = CAP_FACTOR * s_rows // n_exp
    cap_ctx = CAP_FACTOR * l_rows // n_exp
    lay = _in_layout(conv_ch, q_rank, kv_rank, na_cols)
    mla_cols = mla_heads * HEAD_DIM

    o_cq = 0
    o_ckv = o_cq + q_rank
    o_kr = o_ckv + kv_rank
    o_cb = o_kr + MLA_ROPE
    o_cc = o_cb + conv_ch
    o_cx = o_cc + conv_ch
    o_nq = o_cx + conv_ch
    o_nk = o_nq + na_cols
    o_nv = o_nk + na_cols

    n_cond = -(-(b + 1) // 8) * 8
    cond = jnp.zeros((n_cond, d), F32).at[:b].set(c).at[b].set(c_ctx)
    mods = _ada_mod(cond, ada_down, ada_up, ada_bias).reshape(depth, n_cond, N_MOD, d)

    cos_t, sin_t = _rope_tables(s_rows, l_rows)
    tri = jnp.triu(jnp.ones((V7X_LANES, V7X_LANES), BF16), 1)
    ones = jnp.ones((V7X_LANES, V7X_LANES), BF16)
    tok0 = (jnp.arange(b, dtype=jnp.int32) * t_rows)[None, :, None]

    h = jnp.concatenate([x, ctx], axis=1).reshape(b * t_rows, d)

    for l in range(depth):
        m = mods[l]
        lat = lambda i, j: jnp.stack([m[:b, i], m[:b, j]], axis=1)
        cxm = lambda i, j: jnp.stack([m[b, i], m[b, j]], axis=0)

        wl = w_in[l]
        sec ={"cb": wl[:, o_cb:o_cc], "cc": wl[:, o_cc:o_cx], "cx": wl[:, o_cx:o_nq], "cq": wl[:, o_cq:o_ckv],
               "nq": wl[:, o_nq:o_nk], "nk": wl[:, o_nk:o_nv], "nv": wl[:, o_nv:o_nv + na_cols],
               "ckv": wl[:, o_ckv:o_kr], "kr": _spread_rope(wl[:, o_kr:o_cb])}
        w_p = jnp.zeros((d, lay["n"]), BF16)
        for name, blk in sec.items():
            w_p = lax.dynamic_update_slice(w_p, blk.astype(BF16), (0, lay[name]))
        wq = _pad_heads(mla_w_uq[l], mla_heads, nope).astype(BF16)
        wkv = mla_w_ukv[l].reshape(kv_rank, mla_heads, 2 * HEAD_DIM)
        wk = wkv[:, :, :HEAD_DIM].reshape(kv_rank, mla_cols).astype(BF16)
        wv = wkv[:, :, HEAD_DIM:].reshape(kv_rank, mla_cols).T.astype(BF16)
        gq = _pad_heads(mla_q_g[l][None, :], 1, nope)
        gk = _pad_heads(mla_k_g[l][None, :], 1, nope)
        wo = w_out[l].astype(BF16)
        wo_a, wo_b, wo_c = wo[:mla_cols], wo[mla_cols:mla_cols + conv_ch], wo[mla_cols + conv_ch:]
        wr = jnp.zeros((d, V7X_LANES), BF16).at[:, :n_exp].set(w_router[l].astype(BF16))

        px = _proj_in(h, norm1_g[l][None, :], lat(0, 1), cxm(0, 1), w_p, t_rows=t_rows, s_rows=s_rows)
        q, k, vt = _mla_prep(px, lay, mla_qa_g[l][None, :], mla_kva_g[l][None, :], wq, wk, wv, gq, gk, cos_t, sin_t,
                            t_rows=t_rows, heads=mla_heads, qk_dim=qk_dim, scale=qk_dim ** -0.5 * LOG2E)
        q3, k3 = q.reshape(b, t_rows, -1), k.reshape(b, t_rows, -1)
        o_a = jnp.concatenate([
            _flash(q3, k3, vt, heads=mla_heads, q_row0=0, n_q=s_rows, kv_row0=0, n_kv=t_rows,
                   tq=_pick(s_rows, 512, 256), tk=_pick(t_rows, 768, VT_TILE), name="mla_attn"),
            _flash(q3, k3, vt, heads=mla_heads, q_row0=s_rows, n_q=l_rows, kv_row0=s_rows, n_kv=l_rows,
                   tq=l_rows, tk=l_rows, name="mla_attn_ctx")], axis=1)
        o_b = _gated_conv(px, lay, conv_w[l], t_rows=t_rows, s_rows=s_rows)
        o_c = _na_attention(px.reshape(b, t_rows, -1), lay, na_q_g[l][None, :], na_k_g[l][None, :],
                            _na_table(na_rpb[l]), heads=na_heads, s_rows=s_rows, l_rows=l_rows, win_r=win_r,
                            scale=HEAD_DIM ** -0.5 * LOG2E)
        h = _proj_out(o_a.reshape(b * t_rows, -1), o_b, o_c.reshape(b * t_rows, -1), wo_a, wo_b, wo_c, h,
                      m[:b, 2][:, None, :], m[b, 2][None, :], t_rows=t_rows, s_rows=s_rows)

        hn, aff = _moe_norm(h, norm2_g[l][None, :], lat(3, 4), cxm(3, 4), wr, b=b, t_rows=t_rows, s_rows=s_rows,
                            n_exp=n_exp)
        i_lat, g_lat = _topk(aff, tri, ones, row0=0, n=s_rows, k=cap_lat)
        i_ctx, g_ctx = _topk(aff, tri, ones, row0=s_rows, n=l_rows, k=cap_ctx)
        em = lambda a: jnp.transpose(a[:, :, :n_exp], (2, 0, 1))
        idx = jnp.concatenate([em(i_lat) + tok0, em(i_ctx) + tok0 + s_rows], axis=2)
        gates = jnp.concatenate([em(g_lat), em(g_ctx)], axis=2)
        h = _experts(idx[:, :, None, :], gates[..., None], m[:b, 5][:, None, :], m[b, 5][None, :],
                     ex_gate, ex_up, ex_down, l, hn, h, cap_lat=cap_lat)

    return h.reshape(b, t_rows, d)[:, :s_rows]
```

```python
import functools

import jax
import jax.numpy as jnp
import numpy as np
from jax import lax
from jax.experimental import pallas as pl
from jax.experimental.pallas import tpu as pltpu

F32 = jnp.float32
BF16 = jnp.bfloat16
HIGHEST = lax.Precision.HIGHEST

GRID_W = 64
HEAD_DIM = 128
MLA_ROPE = 64
ROPE_F = MLA_ROPE // 4
ROPE_BASE = 10000.0
CAP_FACTOR = 2
EPS = 1e-6
N_MOD = 6
LOG2E = 1.4426950408889634
NEG = -1e30

V7X_LANES = 128
V7X_MXU = 256
V7X_VMEM_LIMIT = 56 * 1024 * 1024
NA_QROWS = 4
NA_KROWS = 12
VT_TILE = 256
FLASH_PAIRS_PER_TRIP = 2


def _pick(n, target, mult):
    best = None
    for d in range(mult, min(n, target) + 1, mult):
        if n % d == 0:
            best = d
    assert best is not None, (n, target, mult)
    return best


def _cparams(sem):
    return pltpu.CompilerParams(dimension_semantics=sem, vmem_limit_bytes=V7X_VMEM_LIMIT)


def _nt_dot(a, b):
    return lax.dot_general(a, b, (((1,), (1,)), ((), ())), preferred_element_type=F32)


def _rms(x, eps):
    return x * lax.rsqrt(jnp.mean(x * x, axis=-1, keepdims=True) + eps)


def _ada_kernel(cond_ref, down_ref, up_ref, bias_ref, out_ref, t_ref):
    @pl.when(pl.program_id(1) == 0)
    def _():
        c = cond_ref[...]
        s = c / (1.0 + jnp.exp(-c))
        t_ref[...] = jnp.dot(s, down_ref[0], precision=HIGHEST, preferred_element_type=F32)

    out_ref[0] = jnp.dot(t_ref[...], up_ref[0], precision=HIGHEST, preferred_element_type=F32) + bias_ref[0]


def _ada_mod(cond, down, up, bias):
    depth, d, rk = down.shape
    n = up.shape[-1]
    tn = _pick(n, 4096, V7X_LANES)
    rows = cond.shape[0]
    return pl.pallas_call(
        _ada_kernel,
        grid=(depth, n // tn),
        in_specs=[
            pl.BlockSpec((rows, d), lambda l, j: (0, 0)),
            pl.BlockSpec((1, d, rk), lambda l, j: (l, 0, 0)),
            pl.BlockSpec((1, rk, tn), lambda l, j: (l, 0, j)),
            pl.BlockSpec((1, 1, tn), lambda l, j: (l, 0, j)),
        ],
        out_specs=pl.BlockSpec((1, rows, tn), lambda l, j: (l, 0, j)),
        out_shape=jax.ShapeDtypeStruct((depth, rows, n), F32),
        scratch_shapes=[pltpu.VMEM((rows, rk), F32)],
        compiler_params=_cparams(("arbitrary", "arbitrary")),
        name="ada_mod",
    )(cond, down, up, bias.reshape(depth, 1, n))


def _modulated_norm(x, g, lat_shift, lat_scale, ctx_shift, ctx_scale, is_ctx, eps):
    y = _rms(x, eps) * g
    shift = jnp.where(is_ctx, ctx_shift, lat_shift)
    scale = jnp.where(is_ctx, ctx_scale, lat_scale)
    return y * (1.0 + scale) + shift


def _row_is_ctx(tile_in_batch, tm, row0, rows, s_rows):
    r = tile_in_batch * tm + row0 + lax.broadcasted_iota(jnp.int32, (rows, 1), 0)
    return r >= s_rows


def _proj_in_kernel(h_ref, g_ref, lat_ref, ctx_ref, w_ref, o_ref, nx_ref, *, tm, tpb, s_rows, chunk, eps):
    @pl.when(pl.program_id(1) == 0)
    def _():
        tile = pl.program_id(0) % tpb

        def body(c, carry):
            r0 = pl.multiple_of(c * chunk, chunk)
            x = h_ref[pl.ds(r0, chunk), :]
            is_ctx = _row_is_ctx(tile, tm, r0, chunk, s_rows)
            y = _modulated_norm(x, g_ref[...], lat_ref[0, 0:1, :], lat_ref[0, 1:2, :],
                                ctx_ref[0:1, :], ctx_ref[1:2, :], is_ctx, eps)
            nx_ref[pl.ds(r0, chunk), :] = y.astype(BF16)
            return carry

        lax.fori_loop(0, tm // chunk, body, 0)

    o_ref[...] = jnp.dot(nx_ref[...], w_ref[...], preferred_element_type=F32).astype(o_ref.dtype)


def _proj_in(h, g, lat_mod, ctx_mod, w, *, t_rows, s_rows):
    r, d = h.shape
    n = w.shape[1]
    tm = _pick(t_rows, 768, 128)
    tn = _pick(n, 512, V7X_LANES)
    tpb = t_rows // tm
    kern = functools.partial(_proj_in_kernel, tm=tm, tpb=tpb, s_rows=s_rows, chunk=_pick(tm, 128, 8), eps=EPS)
    return pl.pallas_call(
        kern,
        grid=(r // tm, n // tn),
        in_specs=[
            pl.BlockSpec((tm, d), lambda i, j: (i, 0)),
            pl.BlockSpec((1, d), lambda i, j: (0, 0)),
            pl.BlockSpec((1, 2, d), lambda i, j: (i // tpb, 0, 0)),
            pl.BlockSpec((2, d), lambda i, j: (0, 0)),
            pl.BlockSpec((d, tn), lambda i, j: (0, j)),
        ],
        out_specs=pl.BlockSpec((tm, tn), lambda i, j: (i, j)),
        out_shape=jax.ShapeDtypeStruct((r, n), BF16),
        scratch_shapes=[pltpu.VMEM((tm, d), BF16)],
        compiler_params=_cparams(("parallel", "arbitrary")),
        name="proj_in",
    )(h, g, lat_mod, ctx_mod, w)


def _rope(r, cos, sin):
    return r * cos + pltpu.roll(r, V7X_LANES // 2, 1) * sin


def _mla_prep_kernel(cq_ref, ckv_ref, kr_ref, gqa_ref, gkva_ref, wq_ref, wk_ref, wv_ref, gq_ref, gk_ref,
                     cos_ref, sin_ref, q_ref, k_ref, vt_ref, *, heads, qk_dim, scale, eps):
    hd, hp = HEAD_DIM, V7X_MXU
    cqn = (_rms(cq_ref[...].astype(F32), eps) * gqa_ref[...]).astype(BF16)
    ckvn = (_rms(ckv_ref[...].astype(F32), eps) * gkva_ref[...]).astype(BF16)
    kr = kr_ref[...].astype(F32)
    cos, sin = cos_ref[...], sin_ref[...]
    kr2 = jnp.sum(kr * kr, axis=-1, keepdims=True)
    vt = _nt_dot(wv_ref[...], ckvn).astype(vt_ref.dtype)
    for c in range(vt_ref.shape[1]):
        vt_ref[0, c] = vt[:, c * VT_TILE:(c + 1) * VT_TILE]
    for h in range(heads):
        a = jnp.dot(cqn, wq_ref[:, h * hp:(h + 1) * hp], preferred_element_type=F32)
        inv = lax.rsqrt(jnp.sum(a * a, axis=-1, keepdims=True) / qk_dim + eps)
        an = a * inv * gq_ref[...]
        q_ref[:, h * hp:h * hp + hd] = (an[:, :hd] * scale).astype(q_ref.dtype)
        q_ref[:, h * hp + hd:(h + 1) * hp] = (_rope(an[:, hd:], cos, sin) * scale).astype(q_ref.dtype)
        kn = jnp.dot(ckvn, wk_ref[:, h * hd:(h + 1) * hd], preferred_element_type=F32)
        invk = lax.rsqrt((jnp.sum(kn * kn, axis=-1, keepdims=True) + kr2) / qk_dim + eps)
        k_ref[:, h * hp:h * hp + hd] = (kn * invk * gk_ref[:, :hd]).astype(k_ref.dtype)
        k_ref[:, h * hp + hd:(h + 1) * hp] = _rope(kr * invk * gk_ref[:, hd:], cos, sin).astype(k_ref.dtype)


def _mla_prep(px, lay, gqa, gkva, wq, wk, wv, gq, gk, cos, sin, *, t_rows, heads, qk_dim, scale):
    r = px.shape[0]
    tm = VT_TILE
    tpb = t_rows // tm
    vpt = tm // VT_TILE
    qr, kvr = gqa.shape[1], gkva.shape[1]
    kern = functools.partial(_mla_prep_kernel, heads=heads, qk_dim=qk_dim, scale=scale, eps=EPS)
    full = lambda a: pl.BlockSpec(a.shape, lambda i: (0,) * a.ndim, pipeline_mode=pl.Buffered(1))
    return pl.pallas_call(
        kern,
        grid=(r // tm,),
        in_specs=[
            pl.BlockSpec((tm, qr), lambda i: (i, lay["cq"] // qr)),
            pl.BlockSpec((tm, kvr), lambda i: (i, lay["ckv"] // kvr)),
            pl.BlockSpec((tm, V7X_LANES), lambda i: (i, lay["kr"] // V7X_LANES)),
            full(gqa), full(gkva), full(wq), full(wk), full(wv), full(gq), full(gk),
            pl.BlockSpec((tm, V7X_LANES), lambda i: (i % tpb, 0)),
            pl.BlockSpec((tm, V7X_LANES), lambda i: (i % tpb, 0)),
        ],
        out_specs=[
            pl.BlockSpec((tm, heads * V7X_MXU), lambda i: (i, 0)),
            pl.BlockSpec((tm, heads * V7X_MXU), lambda i: (i, 0)),
            pl.BlockSpec((1, vpt, heads * HEAD_DIM, VT_TILE), lambda i: (i // tpb, i % tpb, 0, 0)),
        ],
        out_shape=[
            jax.ShapeDtypeStruct((r, heads * V7X_MXU), BF16),
            jax.ShapeDtypeStruct((r, heads * V7X_MXU), BF16),
            jax.ShapeDtypeStruct((r // t_rows, t_rows // VT_TILE, heads * HEAD_DIM, VT_TILE), BF16),
        ],
        compiler_params=_cparams(("parallel",)),
        name="mla_prep",
    )(px, px, px, gqa, gkva, wq, wk, wv, gq, gk, cos, sin)


def _flash_kernel(q_ref, k_ref, vt_ref, o_ref, s0_ref, s1_ref, mx0_ref, mx1_ref, m_ref, l_ref, acc_ref, *, tk, nkv,
                  pairs_per_trip):
    m_ref[...] = jnp.full(m_ref.shape, -jnp.inf, F32)
    l_ref[...] = jnp.zeros(l_ref.shape, F32)
    acc_ref[...] = jnp.zeros(acc_ref.shape, F32)
    vpt = tk // VT_TILE

    def scores(j, s_ref, mx_ref):
        off = pl.multiple_of(j * tk, tk)
        st = _nt_dot(k_ref[0, pl.ds(off, tk), :], q_ref[0])
        s_ref[...] = st
        mx_ref[...] = jnp.max(st, axis=0, keepdims=True)

    def accumulate(j, s_ref, mx_ref):
        m_prev = m_ref[...]
        m_new = jnp.maximum(m_prev, mx_ref[...])
        alpha = jnp.exp2(m_prev - m_new)
        p = jnp.exp2(s_ref[...] - m_new)
        l_ref[...] = alpha * l_ref[...] + jnp.sum(p, axis=0, keepdims=True)
        pb = p.astype(BF16)
        pv = jnp.dot(vt_ref[0, j * vpt], pb[0:VT_TILE], preferred_element_type=F32)
        for c in range(1, vpt):
            pv = pv + jnp.dot(vt_ref[0, j * vpt + c], pb[c * VT_TILE:(c + 1) * VT_TILE],
                              preferred_element_type=F32)
        acc_ref[...] = alpha * acc_ref[...] + pv
        m_ref[...] = m_new

    scores(0, s0_ref, mx0_ref)
    npairs = (nkv - 1) // 2

    def pair(j):
        scores(j + 1, s1_ref, mx1_ref)
        accumulate(j, s0_ref, mx0_ref)
        scores(j + 2, s0_ref, mx0_ref)
        accumulate(j + 1, s1_ref, mx1_ref)

    def body(jj, carry):
        for u in range(pairs_per_trip):
            pair(2 * (jj * pairs_per_trip + u))
        return carry

    ntrips = npairs // pairs_per_trip
    lax.fori_loop(0, ntrips, body, 0)
    for pr in range(ntrips * pairs_per_trip, npairs):
        pair(2 * pr)
    if (nkv - 1) % 2 == 0:
        accumulate(nkv - 1, s0_ref, mx0_ref)
    else:
        scores(nkv - 1, s1_ref, mx1_ref)
        accumulate(nkv - 2, s0_ref, mx0_ref)
        accumulate(nkv - 1, s1_ref, mx1_ref)
    o_ref[0] = (acc_ref[...] / l_ref[...]).T.astype(o_ref.dtype)


def _flash(q, k, vt, *, heads, q_row0, n_q, kv_row0, n_kv, tq, tk, name, pairs_per_trip=FLASH_PAIRS_PER_TRIP):
    b = q.shape[0]
    dk, dv = V7X_MXU, HEAD_DIM
    assert q_row0 % tq == 0 and n_q % tq == 0 and kv_row0 % n_kv == 0 and n_kv % tk == 0
    assert tk % VT_TILE == 0
    kern = functools.partial(_flash_kernel, tk=tk, nkv=n_kv // tk, pairs_per_trip=pairs_per_trip)
    return pl.pallas_call(
        kern,
        grid=(b, heads, n_q // tq),
        in_specs=[
            pl.BlockSpec((1, tq, dk), lambda bi, h, i: (bi, q_row0 // tq + i, h)),
            pl.BlockSpec((1, n_kv, dk), lambda bi, h, i: (bi, kv_row0 // n_kv, h)),
            pl.BlockSpec((1, n_kv // VT_TILE, dv, VT_TILE), lambda bi, h, i: (bi, kv_row0 // n_kv, h, 0)),
        ],
        out_specs=pl.BlockSpec((1, tq, dv), lambda bi, h, i: (bi, i, h)),
        out_shape=jax.ShapeDtypeStruct((b, n_q, heads * dv), BF16),
        scratch_shapes=[pltpu.VMEM((tk, tq), F32), pltpu.VMEM((tk, tq), F32),
                        pltpu.VMEM((1, tq), F32), pltpu.VMEM((1, tq), F32),
                        pltpu.VMEM((1, tq), F32), pltpu.VMEM((1, tq), F32), pltpu.VMEM((dv, tq), F32)],
        compiler_params=_cparams(("parallel", "parallel", "arbitrary")),
        name=name,
    )(q, k, vt)


def _na_table_kernel(rpb_ref, tbl_ref, *, win_r, win_c, w):
    h = pl.program_id(0)
    q = lax.broadcasted_iota(jnp.int32, (w, 2 * w), 0)
    lane = lax.broadcasted_iota(jnp.int32, (w, 2 * w), 1)
    c0 = jnp.clip(q - win_c // 2, 0, w - win_c)
    n_d = 2 * win_r - 1
    for side in range(2):
        kc = lane - side * w
        inside = (kc >= c0) & (kc < c0 + win_c)
        own_half = (kc >= 0) & (kc < w)
        rel = kc - q + (win_c - 1)
        for d in range(n_d):
            acc = jnp.full((w, 2 * w), NEG, F32)
            for dc in range(2 * win_c - 1):
                acc = jnp.where(inside & (rel == dc), rpb_ref[h, d, dc] * LOG2E, acc)
            tbl_ref[0, side, d] = jnp.where(own_half, acc, 0.0)
        tbl_ref[0, side, n_d] = jnp.where(own_half, NEG, 0.0)


def _na_table(rpb):
    heads, n_d, n_c = rpb.shape
    win_r, win_c = (n_d + 1) // 2, (n_c + 1) // 2
    kern = functools.partial(_na_table_kernel, win_r=win_r, win_c=win_c, w=GRID_W)
    return pl.pallas_call(
        kern,
        grid=(heads,),
        in_specs=[pl.BlockSpec(memory_space=pltpu.SMEM)],
        out_specs=pl.BlockSpec((1, 2, n_d + 1, GRID_W, 2 * GRID_W), lambda h: (h, 0, 0, 0, 0)),
        out_shape=jax.ShapeDtypeStruct((heads, 2, n_d + 1, GRID_W, 2 * GRID_W), F32),
        compiler_params=_cparams(("arbitrary",)),
        name="na_table",
    )(rpb)


def _na_kernel(q_ref, k_ref, v_ref, gq_ref, gk_ref, tbl_ref, o_ref, kn_ref, bias_ref, sl0_ref, sc0_ref,
               sl1_ref, sc1_ref, *, s_rows, l_rows, grid_rows, win_r, scale, eps, chunk):
    w = GRID_W
    t_rows = s_rows + l_rows
    nq, nk = NA_QROWS * w, NA_KROWS * w

    def kn_body(c, carry):
        r0 = pl.multiple_of(c * chunk, chunk)
        kk = k_ref[0, pl.ds(r0, chunk), :].astype(F32)
        kn_ref[pl.ds(r0, chunk), :] = (_rms(kk, eps) * gk_ref[...]).astype(BF16)
        return carry

    lax.fori_loop(0, t_rows // chunk, kn_body, 0)
    kc = kn_ref[s_rows:t_rows, :]
    vc = v_ref[0, s_rows:t_rows, :]

    def norm_q(x):
        return (_rms(x.astype(F32), eps) * gq_ref[...] * scale).astype(BF16)

    def softmax_pv(parts):
        m = functools.reduce(jnp.maximum, [jnp.max(s, axis=-1, keepdims=True) for s, _ in parts])
        ps = [jnp.exp2(s - m) for s, _ in parts]
        den = functools.reduce(lambda a, b: a + b, [jnp.sum(p, axis=-1, keepdims=True) for p in ps])
        num = functools.reduce(lambda a, b: a + b,
                               [jnp.dot(p.astype(BF16), vv, preferred_element_type=F32)
                                for p, (_, vv) in zip(ps, parts)])
        return num / den

    def key_start(i):
        return jnp.clip(i * NA_QROWS - win_r // 2, 0, grid_rows - NA_KROWS)

    def scores(i, sl_ref, sc_ref):
        q0 = pl.multiple_of(i * nq, nq)
        ks = key_start(i)
        k0 = pl.multiple_of(ks * w, w)
        qn = norm_q(q_ref[0, pl.ds(q0, nq), :])
        for qr in range(NA_QROWS):
            r = i * NA_QROWS + qr
            r0 = jnp.clip(r - win_r // 2, 0, grid_rows - win_r)
            for p in range(NA_KROWS // 2):
                ds = []
                for side in range(2):
                    kr = ks + 2 * p + side
                    ok = (kr >= r0) & (kr < r0 + win_r)
                    ds.append(jnp.where(ok, kr - r + (win_r - 1), 2 * win_r - 1))
                bias_ref[qr * w:(qr + 1) * w, p * 2 * w:(p + 1) * 2 * w] = (
                    tbl_ref[0, 0, ds[0]] + tbl_ref[0, 1, ds[1]])
        sl_ref[...] = _nt_dot(qn, kn_ref[pl.ds(k0, nk), :]) + bias_ref[...]
        sc_ref[...] = _nt_dot(qn, kc)

    def attend(i, sl_ref, sc_ref):
        q0 = pl.multiple_of(i * nq, nq)
        k0 = pl.multiple_of(key_start(i) * w, w)
        o = softmax_pv([(sc_ref[...], vc), (sl_ref[...], v_ref[0, pl.ds(k0, nk), :])])
        o_ref[0, pl.ds(q0, nq), :] = o.astype(o_ref.dtype)

    nblk = grid_rows // NA_QROWS
    scores(0, sl0_ref, sc0_ref)
    npairs = (nblk - 1) // 2

    def pair_body(jj, carry):
        i = 2 * jj
        scores(i + 1, sl1_ref, sc1_ref)
        attend(i, sl0_ref, sc0_ref)
        scores(i + 2, sl0_ref, sc0_ref)
        attend(i + 1, sl1_ref, sc1_ref)
        return carry

    lax.fori_loop(0, npairs, pair_body, 0)
    if (nblk - 1) % 2 == 0:
        attend(nblk - 1, sl0_ref, sc0_ref)
    else:
        scores(nblk - 1, sl1_ref, sc1_ref)
        attend(nblk - 2, sl0_ref, sc0_ref)
        attend(nblk - 1, sl1_ref, sc1_ref)

    qc = norm_q(q_ref[0, s_rows:t_rows, :])
    o_ref[0, s_rows:t_rows, :] = softmax_pv([(_nt_dot(qc, kc), vc)]).astype(o_ref.dtype)


def _na_attention(px3, lay, gq, gk, tbl, *, heads, s_rows, l_rows, win_r, scale):
    b, t, _ = px3.shape
    hd = HEAD_DIM
    grid_rows = s_rows // GRID_W
    assert 2 * GRID_W == V7X_LANES and grid_rows % NA_QROWS == 0 and grid_rows >= NA_KROWS
    assert NA_KROWS >= NA_QROWS + win_r - 1 and NA_KROWS % 2 == 0
    kern = functools.partial(_na_kernel, s_rows=s_rows, l_rows=l_rows, grid_rows=grid_rows, win_r=win_r,
                             scale=scale, eps=EPS, chunk=_pick(t, 256, 16))
    col = lambda name: (lambda bi, h: (bi, 0, lay[name] // hd + h))
    return pl.pallas_call(
        kern,
        grid=(b, heads),
        in_specs=[
            pl.BlockSpec((1, t, hd), col("nq")),
            pl.BlockSpec((1, t, hd), col("nk")),
            pl.BlockSpec((1, t, hd), col("nv")),
            pl.BlockSpec((1, hd), lambda bi, h: (0, 0)),
            pl.BlockSpec((1, hd), lambda bi, h: (0, 0)),
            pl.BlockSpec((1,) + tbl.shape[1:], lambda bi, h: (h, 0, 0, 0, 0)),
        ],
        out_specs=pl.BlockSpec((1, t, hd), lambda bi, h: (bi, 0, h)),
        out_shape=jax.ShapeDtypeStruct((b, t, heads * hd), BF16),
        scratch_shapes=[pltpu.VMEM((t, hd), BF16), pltpu.VMEM((NA_QROWS * GRID_W, NA_KROWS * GRID_W), F32),
                        pltpu.VMEM((NA_QROWS * GRID_W, NA_KROWS * GRID_W), F32),
                        pltpu.VMEM((NA_QROWS * GRID_W, l_rows), F32),
                        pltpu.VMEM((NA_QROWS * GRID_W, NA_KROWS * GRID_W), F32),
                        pltpu.VMEM((NA_QROWS * GRID_W, l_rows), F32)],
        compiler_params=_cparams(("parallel", "parallel")),
        name="na_attn",
    )(px3, px3, px3, gq, gk, tbl)


def _conv_kernel(cb_ref, cc_ref, cx_ref, ccp_ref, cxp_ref, ccn_ref, cxn_ref, w_ref, o_ref, *,
                 tm, tpb, s_rows, t_rows, halo):
    u = cc_ref[...].astype(F32) * cx_ref[...].astype(F32)
    u_before = ccp_ref[halo - 1:halo, :].astype(F32) * cxp_ref[halo - 1:halo, :].astype(F32)
    u_after = ccn_ref[0:1, :].astype(F32) * cxn_ref[0:1, :].astype(F32)
    loc = lax.broadcasted_iota(jnp.int32, (tm, 1), 0)
    row = (pl.program_id(0) % tpb) * tm + loc
    first = (row == 0) | (row == s_rows)
    last = (row == s_rows - 1) | (row == t_rows - 1)
    u_prev = jnp.where(loc == 0, u_before, pltpu.roll(u, 1, 0))
    u_prev = jnp.where(first, 0.0, u_prev)
    u_next = jnp.where(loc == tm - 1, u_after, pltpu.roll(u, tm - 1, 0))
    u_next = jnp.where(last, 0.0, u_next)
    y = w_ref[0:1, :] * u_prev + w_ref[1:2, :] * u + w_ref[2:3, :] * u_next
    o_ref[...] = (cb_ref[...].astype(F32) * y).astype(o_ref.dtype)


def _gated_conv(px, lay, w, *, t_rows, s_rows):
    r = px.shape[0]
    c = w.shape[1]
    halo = 16
    tm = _pick(t_rows, 256, 128)
    tpb = t_rows // tm
    nb = r // halo
    kern = functools.partial(_conv_kernel, tm=tm, tpb=tpb, s_rows=s_rows, t_rows=t_rows, halo=halo)
    blk = lambda name: pl.BlockSpec((tm, c), lambda i: (i, lay[name] // c))
    prev = lambda name: pl.BlockSpec((halo, c), lambda i: (jnp.maximum(i * (tm // halo) - 1, 0), lay[name] // c))
    nxt = lambda name: pl.BlockSpec((halo, c), lambda i: (jnp.minimum((i + 1) * (tm // halo), nb - 1), lay[name] // c))
    return pl.pallas_call(
        kern,
        grid=(r // tm,),
        in_specs=[blk("cb"), blk("cc"), blk("cx"), prev("cc"), prev("cx"), nxt("cc"), nxt("cx"),
                  pl.BlockSpec(w.shape, lambda i: (0, 0))],
        out_specs=pl.BlockSpec((tm, c), lambda i: (i, 0)),
        out_shape=jax.ShapeDtypeStruct((r, c), BF16),
        compiler_params=_cparams(("parallel",)),
        name="gated_conv",
    )(px, px, px, px, px, px, px, w)


def _proj_out_kernel(oa_ref, ob_ref, oc_ref, wa_ref, wb_ref, wc_ref, h_ref, lat_ref, ctx_ref, o_ref, *,
                     tm, tpb, s_rows):
    y = jnp.dot(oa_ref[...], wa_ref[...], preferred_element_type=F32)
    y = y + jnp.dot(ob_ref[...], wb_ref[...], preferred_element_type=F32)
    y = y + jnp.dot(oc_ref[...], wc_ref[...], preferred_element_type=F32)
    is_ctx = _row_is_ctx(pl.program_id(0) % tpb, tm, 0, tm, s_rows)
    gate = jnp.where(is_ctx, ctx_ref[...], lat_ref[0])
    o_ref[...] = h_ref[...] + gate * y


def _proj_out(oa, ob, oc, wa, wb, wc, h, lat_gate, ctx_gate, *, t_rows, s_rows):
    r, d = h.shape
    tm = _pick(t_rows, 768, 128)
    tn = _pick(d, 1024, V7X_LANES)
    tpb = t_rows // tm
    kern = functools.partial(_proj_out_kernel, tm=tm, tpb=tpb, s_rows=s_rows)
    lhs = lambda a: pl.BlockSpec((tm, a.shape[1]), lambda i, j: (i, 0))
    rhs = lambda a: pl.BlockSpec((a.shape[0], tn), lambda i, j: (0, j))
    return pl.pallas_call(
        kern,
        grid=(r // tm, d // tn),
        in_specs=[lhs(oa), lhs(ob), lhs(oc), rhs(wa), rhs(wb), rhs(wc),
                  pl.BlockSpec((tm, tn), lambda i, j: (i, j)),
                  pl.BlockSpec((1, 1, tn), lambda i, j: (i // tpb, 0, j)),
                  pl.BlockSpec((1, tn), lambda i, j: (0, j))],
        out_specs=pl.BlockSpec((tm, tn), lambda i, j: (i, j)),
        out_shape=jax.ShapeDtypeStruct((r, d), F32),
        input_output_aliases={6: 0},
        compiler_params=_cparams(("parallel", "arbitrary")),
        name="proj_out",
    )(oa, ob, oc, wa, wb, wc, h, lat_gate, ctx_gate)


def _moe_norm_kernel(h_ref, g_ref, lat_ref, ctx_ref, wr_ref, hn_ref, aff_ref, *, tm, tpb, s_rows, n_exp, eps):
    is_ctx = _row_is_ctx(pl.program_id(0) % tpb, tm, 0, tm, s_rows)
    hn = _modulated_norm(h_ref[...], g_ref[...], lat_ref[0, 0:1, :], lat_ref[0, 1:2, :],
                         ctx_ref[0:1, :], ctx_ref[1:2, :], is_ctx, eps)
    hb = hn.astype(BF16)
    half = hn.shape[1] // 2
    hf = hb.astype(F32)
    hn_ref[...] = pltpu.bitcast(hf[:, :half], jnp.uint32) | (pltpu.bitcast(hf[:, half:], jnp.uint32) >> 16)
    logits = jnp.dot(hb, wr_ref[...], preferred_element_type=F32)
    lane = lax.broadcasted_iota(jnp.int32, logits.shape, 1)
    logits = jnp.where(lane < n_exp, logits, -jnp.inf)
    e = jnp.exp(logits - jnp.max(logits, axis=-1, keepdims=True))
    aff = e / jnp.sum(e, axis=-1, keepdims=True)
    aff_ref[0] = aff.T[:n_exp, :]


def _moe_norm(h, g, lat_mod, ctx_mod, wr, *, b, t_rows, s_rows, n_exp):
    r, d = h.shape
    tm = _pick(t_rows, 256, 128)
    tpb = t_rows // tm
    kern = functools.partial(_moe_norm_kernel, tm=tm, tpb=tpb, s_rows=s_rows, n_exp=n_exp, eps=EPS)
    return pl.pallas_call(
        kern,
        grid=(r // tm,),
        in_specs=[
            pl.BlockSpec((tm, d), lambda i: (i, 0)),
            pl.BlockSpec((1, d), lambda i: (0, 0)),
            pl.BlockSpec((1, 2, d), lambda i: (i // tpb, 0, 0)),
            pl.BlockSpec((2, d), lambda i: (0, 0)),
            pl.BlockSpec(wr.shape, lambda i: (0, 0)),
        ],
        out_specs=[pl.BlockSpec((tm, d // 2), lambda i: (i, 0)),
                   pl.BlockSpec((1, n_exp, tm), lambda i: (i // tpb, 0, i % tpb))],
        out_shape=[jax.ShapeDtypeStruct((r, d // 2), jnp.uint32), jax.ShapeDtypeStruct((b, n_exp, t_rows), F32)],
        compiler_params=_cparams(("parallel",)),
        name="moe_norm",
    )(h, g, lat_mod, ctx_mod, wr)


def _topk_kernel(aff_ref, tri_ref, ones_ref, idx_ref, gate_ref, posm_ref, *, n, k, n_exp):
    ln = V7X_LANES
    nt = n // ln
    a = aff_ref[0]
    bits = pltpu.bitcast(a, jnp.int32)
    thr = jnp.zeros((n_exp, 1), jnp.int32)
    for bit in range(30, -1, -1):
        cand = thr | (1 << bit)
        cnt = jnp.sum((bits >= cand).astype(jnp.int32), axis=1, keepdims=True)
        thr = jnp.where(cnt >= k, cand, thr)
    gt = bits > thr
    eq = bits == thr

    def excl_prefix(x):
        xb = x.astype(BF16)
        run = jnp.zeros((n_exp, ln), F32)
        outs = []
        for t in range(nt):
            blk = xb[:, t * ln:(t + 1) * ln]
            outs.append(jnp.dot(blk, tri_ref[...], preferred_element_type=F32) + run)
            run = run + jnp.dot(blk, ones_ref[...], preferred_element_type=F32)
        return jnp.concatenate(outs, axis=1)

    need = (k - jnp.sum(gt.astype(jnp.int32), axis=1, keepdims=True)).astype(F32)
    eqf = jnp.where(eq, 1.0, 0.0)
    sel = gt | (eq & (excl_prefix(eqf) < need))
    pos = excl_prefix(jnp.where(sel, 1.0, 0.0))
    posm_ref[...] = jnp.where(sel, pos, -1.0)

    lane = lax.broadcasted_iota(jnp.int32, (8, ln), 1)
    lane_f = lane.astype(F32)

    def jb_body(jb, carry):
        slot = (jb * 8 + lax.broadcasted_iota(jnp.int32, (8, 1), 0)).astype(F32)
        out_i = jnp.zeros((8, ln), F32)
        out_g = jnp.zeros((8, ln), F32)
        for e in range(n_exp):
            acc_i = jnp.zeros((8, ln), F32)
            acc_g = jnp.zeros((8, ln), F32)
            for t in range(nt):
                hit = posm_ref[e:e + 1, t * ln:(t + 1) * ln] == slot
                acc_i = acc_i + jnp.where(hit, lane_f + float(t * ln), 0.0)
                acc_g = acc_g + jnp.where(hit, aff_ref[0, e:e + 1, t * ln:(t + 1) * ln], 0.0)
            out_i = jnp.where(lane == e, jnp.sum(acc_i, axis=1, keepdims=True), out_i)
            out_g = jnp.where(lane == e, jnp.sum(acc_g, axis=1, keepdims=True), out_g)
        r0 = pl.multiple_of(jb * 8, 8)
        idx_ref[0, pl.ds(r0, 8), :] = out_i.astype(jnp.int32)
        gate_ref[0, pl.ds(r0, 8), :] = out_g
        return carry

    lax.fori_loop(0, k // 8, jb_body, 0)


def _topk(aff, tri, ones, *, row0, n, k):
    b, n_exp, _ = aff.shape
    assert row0 % n == 0 and n % V7X_LANES == 0 and k % 8 == 0 and n_exp <= V7X_LANES
    kern = functools.partial(_topk_kernel, n=n, k=k, n_exp=n_exp)
    return pl.pallas_call(
        kern,
        grid=(b,),
        in_specs=[pl.BlockSpec((1, n_exp, n), lambda bi: (bi, 0, row0 // n)),
                  pl.BlockSpec(tri.shape, lambda bi: (0, 0)),
                  pl.BlockSpec(ones.shape, lambda bi: (0, 0))],
        out_specs=[pl.BlockSpec((1, k, V7X_LANES), lambda bi: (bi, 0, 0)),
                   pl.BlockSpec((1, k, V7X_LANES), lambda bi: (bi, 0, 0))],
        out_shape=[jax.ShapeDtypeStruct((b, k, V7X_LANES), jnp.int32),
                   jax.ShapeDtypeStruct((b, k, V7X_LANES), F32)],
        scratch_shapes=[pltpu.VMEM((n_exp, n), F32)],
        compiler_params=_cparams(("parallel",)),
        name="topk",
    )(aff, tri, ones)


ROW_UNROLL = 16
SEM_HN, SEM_H, SEM_OUT = 0, 2, 3


def _expert_kernel(idx_ref, g_ref, lat_ref, ctx_ref, wg_ref, wu_ref, wd_ref, hn_hbm, h_hbm, out_hbm,
                   xe_ref, y_ref, st_ref, sem, *, rows, cap_lat, dcols, nchunk, unroll):
    del h_hbm
    e, bi, f = pl.program_id(0), pl.program_id(1), pl.program_id(2)
    ne, nb, nf = pl.num_programs(0), pl.num_programs(1), pl.num_programs(2)
    ch = rows // nchunk
    half = xe_ref.shape[1] // 2

    def issue_rows(n_rows, make):
        def trip(jj, carry):
            base = pl.multiple_of(jj * unroll, unroll)
            for u in range(unroll):
                make(base, u).start()
            return carry

        lax.fori_loop(0, n_rows // unroll, trip, 0)

    def hn_row(c, base, u):
        return pltpu.make_async_copy(hn_hbm.at[pl.ds(idx_ref[0, 0, 0, c * ch + base + u], 1)],
                                     st_ref.at[c % 2, pl.ds(base + u, 1)], sem.at[SEM_HN + c % 2])

    def hn_chunk_all(c):
        return pltpu.make_async_copy(hn_hbm.at[pl.ds(0, ch)], st_ref.at[c % 2], sem.at[SEM_HN + c % 2])

    def h_row(base, u):
        return pltpu.make_async_copy(out_hbm.at[pl.ds(idx_ref[0, 0, 0, base + u], 1)],
                                     y_ref.at[pl.ds(base + u, 1)], sem.at[SEM_H])

    def out_row(base, u):
        return pltpu.make_async_copy(y_ref.at[pl.ds(base + u, 1)],
                                     out_hbm.at[pl.ds(idx_ref[0, 0, 0, base + u], 1)], sem.at[SEM_OUT])

    def all_rows(s):
        return pltpu.make_async_copy(out_hbm.at[pl.ds(0, rows)], y_ref, sem.at[s])

    @pl.when(f == 0)
    def _():
        issue_rows(ch, functools.partial(hn_row, 0))

        @pl.when((e > 0) | (bi > 0))
        def _():
            all_rows(SEM_OUT).wait()

        issue_rows(rows, h_row)
        for c in range(nchunk):
            if c + 1 < nchunk:
                issue_rows(ch, functools.partial(hn_row, c + 1))
            hn_chunk_all(c).wait()
            p = st_ref[c % 2]
            hi = pltpu.bitcast(p & jnp.uint32(0xFFFF0000), F32)
            lo = pltpu.bitcast(p << 16, F32)
            xe_ref[c * ch:(c + 1) * ch, :half] = hi.astype(BF16)
            xe_ref[c * ch:(c + 1) * ch, half:] = lo.astype(BF16)

    xe = xe_ref[...]
    a = jnp.dot(xe, wg_ref[0, 0].astype(BF16), preferred_element_type=F32)
    u = jnp.dot(xe, wu_ref[0, 0].astype(BF16), preferred_element_type=F32)
    hm = ((a / (1.0 + jnp.exp(-a))) * u).astype(BF16)

    @pl.when(f == 0)
    def _():
        all_rows(SEM_H).wait()

    g = g_ref[0, 0]
    d = y_ref.shape[1]
    for c in range(d // dcols):
        cs = slice(c * dcols, (c + 1) * dcols)
        y = jnp.dot(hm, wd_ref[0, 0, :, cs].astype(BF16), preferred_element_type=F32) * g
        y_ref[0:cap_lat, cs] = y_ref[0:cap_lat, cs] + lat_ref[0, :, cs] * y[0:cap_lat]
        y_ref[cap_lat:rows, cs] = y_ref[cap_lat:rows, cs] + ctx_ref[:, cs] * y[cap_lat:rows]

    @pl.when(f == nf - 1)
    def _():
        issue_rows(rows, out_row)

        @pl.when((e == ne - 1) & (bi == nb - 1))
        def _():
            all_rows(SEM_OUT).wait()


def _experts(idx, gates, lat_gate, ctx_gate, wg, wu, wd, layer, hn, h, *, cap_lat, unroll=ROW_UNROLL):
    n_exp, b, _, rows = idx.shape
    d, ff = wg.shape[2], wg.shape[3]
    fc = _pick(ff, 256, V7X_LANES)
    nchunk = max(n for n in range(1, 7) if rows % (16 * n) == 0)
    assert rows % 16 == 0 and hn.shape[1] * 2 == d
    kern = functools.partial(_expert_kernel, rows=rows, cap_lat=cap_lat, dcols=_pick(d, 1024, V7X_LANES),
                             nchunk=nchunk, unroll=unroll)
    assert (rows // nchunk) % unroll == 0
    return pl.pallas_call(
        kern,
        grid=(n_exp, b, ff // fc),
        in_specs=[
            pl.BlockSpec((1, 1, 1, rows), lambda e, bi, f: (e, bi, 0, 0), memory_space=pltpu.SMEM),
            pl.BlockSpec((1, 1, rows, 1), lambda e, bi, f: (e, bi, 0, 0)),
            pl.BlockSpec((1, 1, d), lambda e, bi, f: (bi, 0, 0)),
            pl.BlockSpec((1, d), lambda e, bi, f: (0, 0)),
            pl.BlockSpec((1, 1, d, fc), lambda e, bi, f: (layer, e, 0, f)),
            pl.BlockSpec((1, 1, d, fc), lambda e, bi, f: (layer, e, 0, f)),
            pl.BlockSpec((1, 1, fc, d), lambda e, bi, f: (layer, e, f, 0)),
            pl.BlockSpec(memory_space=pl.ANY),
            pl.BlockSpec(memory_space=pl.ANY),
        ],
        out_specs=pl.BlockSpec(memory_space=pl.ANY),
        out_shape=jax.ShapeDtypeStruct(h.shape, F32),
        scratch_shapes=[pltpu.VMEM((rows, d), BF16), pltpu.VMEM((rows, d), F32),
                        pltpu.VMEM((2, rows // nchunk, d // 2), jnp.uint32), pltpu.SemaphoreType.DMA((4,))],
        input_output_aliases={8: 0},
        compiler_params=_cparams(("arbitrary", "arbitrary", "arbitrary")),
        name="experts",
    )(idx, gates, lat_gate, ctx_gate, wg, wu, wd, hn, h)


def _in_layout(conv_ch, q_rank, kv_rank, na_cols):
    lay, off = {}, 0

    def put(name, width, align):
        nonlocal off
        off = -(-off // align) * align
        lay[name] = off
        off += width

    for name in ("cb", "cc", "cx"):
        put(name, conv_ch, conv_ch)
    put("cq", q_rank, q_rank)
    for name in ("nq", "nk", "nv"):
        put(name, na_cols, HEAD_DIM)
    put("ckv", kv_rank, kv_rank)
    put("kr", V7X_LANES, V7X_LANES)
    lay["n"] = -(-off // 512) * 512
    return lay


def _spread_rope(a):
    half = MLA_ROPE // 2
    z = jnp.zeros(a.shape[:-1] + (V7X_LANES // 2 - half,), a.dtype)
    return jnp.concatenate([a[..., :half], z, a[..., half:], z], axis=-1)


def _pad_heads(a, heads, nope):
    a = a.reshape(a.shape[:-1] + (heads, nope + MLA_ROPE))
    a = jnp.concatenate([a[..., :nope], _spread_rope(a[..., nope:])], axis=-1)
    return a.reshape(a.shape[:-2] + (heads * V7X_MXU,))


def _rope_tables(s_rows, l_rows):
    pos = jnp.arange(s_rows, dtype=jnp.int32)
    row = (pos // GRID_W).astype(F32)
    col = (pos % GRID_W).astype(F32)
    inv = jnp.power(ROPE_BASE, -jnp.arange(ROPE_F, dtype=F32) / ROPE_F)
    ang = jnp.concatenate([row[:, None] * inv, col[:, None] * inv], axis=1)
    cos = jnp.concatenate([jnp.cos(ang), jnp.ones((l_rows, 2 * ROPE_F), F32)], axis=0)
    sin = jnp.concatenate([jnp.sin(ang), jnp.zeros((l_rows, 2 * ROPE_F), F32)], axis=0)
    return _spread_rope(jnp.concatenate([cos, cos], axis=1)), _spread_rope(jnp.concatenate([-sin, sin], axis=1))


def kernel(x, c, ctx, c_ctx, ada_down, ada_up, ada_bias, norm1_g, w_in, mla_qa_g, mla_kva_g, mla_w_uq, mla_w_ukv, mla_q_g, mla_k_g, conv_w, na_q_g, na_k_g, na_rpb, w_out, norm2_g, w_router, ex_gate, ex_up, ex_down):
    b, s_rows, d = x.shape
    l_rows = ctx.shape[1]
    t_rows = s_rows + l_rows
    depth = w_in.shape[0]
    q_rank, kv_rank = mla_qa_g.shape[1], mla_kva_g.shape[1]
    qk_dim = mla_q_g.shape[1]
    nope = qk_dim - MLA_ROPE
    assert nope == HEAD_DIM
    mla_heads = mla_w_uq.shape[2] // qk_dim
    conv_ch = conv_w.shape[2]
    na_heads = na_rpb.shape[1]
    na_cols = na_heads * HEAD_DIM
    win_r = (na_rpb.shape[2] + 1) // 2
    n_exp = w_router.shape[2]
    cap_lat = CAP_FACTOR * s_rows // n_exp
    cap_ctx = CAP_FACTOR * l_rows // n_exp
    lay = _in_layout(conv_ch, q_rank, kv_rank, na_cols)
    mla_cols = mla_heads * HEAD_DIM

    o_cq = 0
    o_ckv = o_cq + q_rank
    o_kr = o_ckv + kv_rank
    o_cb = o_kr + MLA_ROPE
    o_cc = o_cb + conv_ch
    o_cx = o_cc + conv_ch
    o_nq = o_cx + conv_ch
    o_nk = o_nq + na_cols
    o_nv = o_nk + na_cols

    n_cond = -(-(b + 1) // 8) * 8
    cond = jnp.zeros((n_cond, d), F32).at[:b].set(c).at[b].set(c_ctx)
    mods = _ada_mod(cond, ada_down, ada_up, ada_bias).reshape(depth, n_cond, N_MOD, d)

    cos_t, sin_t = _rope_tables(s_rows, l_rows)
    tri = jnp.triu(jnp.ones((V7X_LANES, V7X_LANES), BF16), 1)
    ones = jnp.ones((V7X_LANES, V7X_LANES), BF16)
    tok0 = (jnp.arange(b, dtype=jnp.int32) * t_rows)[None, :, None]

    h = jnp.concatenate([x, ctx], axis=1).reshape(b * t_rows, d)

    for l in range(depth):
        m = mods[l]
        lat = lambda i, j: jnp.stack([m[:b, i], m[:b, j]], axis=1)
        cxm = lambda i, j: jnp.stack([m[b, i], m[b, j]], axis=0)

        wl = w_in[l]
        sec ={"cb": wl[:, o_cb:o_cc], "cc": wl[:, o_cc:o_cx], "cx": wl[:, o_cx:o_nq], "cq": wl[:, o_cq:o_ckv],
               "nq": wl[:, o_nq:o_nk], "nk": wl[:, o_nk:o_nv], "nv": wl[:, o_nv:o_nv + na_cols],
               "ckv": wl[:, o_ckv:o_kr], "kr": _spread_rope(wl[:, o_kr:o_cb])}
        w_p = jnp.zeros((d, lay["n"]), BF16)
        for name, blk in sec.items():
            w_p = lax.dynamic_update_slice(w_p, blk.astype(BF16), (0, lay[name]))
        wq = _pad_heads(mla_w_uq[l], mla_heads, nope).astype(BF16)
        wkv = mla_w_ukv[l].reshape(kv_rank, mla_heads, 2 * HEAD_DIM)
        wk = wkv[:, :, :HEAD_DIM].reshape(kv_rank, mla_cols).astype(BF16)
        wv = wkv[:, :, HEAD_DIM:].reshape(kv_rank, mla_cols).T.astype(BF16)
        gq = _pad_heads(mla_q_g[l][None, :], 1, nope)
        gk = _pad_heads(mla_k_g[l][None, :], 1, nope)
        wo = w_out[l].astype(BF16)
        wo_a, wo_b, wo_c = wo[:mla_cols], wo[mla_cols:mla_cols + conv_ch], wo[mla_cols + conv_ch:]
        wr = jnp.zeros((d, V7X_LANES), BF16).at[:, :n_exp].set(w_router[l].astype(BF16))

        px = _proj_in(h, norm1_g[l][None, :], lat(0, 1), cxm(0, 1), w_p, t_rows=t_rows, s_rows=s_rows)
        q, k, vt = _mla_prep(px, lay, mla_qa_g[l][None, :], mla_kva_g[l][None, :], wq, wk, wv, gq, gk, cos_t, sin_t,
                            t_rows=t_rows, heads=mla_heads, qk_dim=qk_dim, scale=qk_dim ** -0.5 * LOG2E)
        q3, k3 = q.reshape(b, t_rows, -1), k.reshape(b, t_rows, -1)
        o_a = jnp.concatenate([
            _flash(q3, k3, vt, heads=mla_heads, q_row0=0, n_q=s_rows, kv_row0=0, n_kv=t_rows,
                   tq=_pick(s_rows, 2048, 256), tk=_pick(t_rows, 768, VT_TILE), name="mla_attn"),
            _flash(q3, k3, vt, heads=mla_heads, q_row0=s_rows, n_q=l_rows, kv_row0=s_rows, n_kv=l_rows,
                   tq=l_rows, tk=l_rows, name="mla_attn_ctx")], axis=1)
        o_b = _gated_conv(px, lay, conv_w[l], t_rows=t_rows, s_rows=s_rows)
        o_c = _na_attention(px.reshape(b, t_rows, -1), lay, na_q_g[l][None, :], na_k_g[l][None, :],
                            _na_table(na_rpb[l]), heads=na_heads, s_rows=s_rows, l_rows=l_rows, win_r=win_r,
                            scale=HEAD_DIM ** -0.5 * LOG2E)
        h = _proj_out(o_a.reshape(b * t_rows, -1), o_b, o_c.reshape(b * t_rows, -1), wo_a, wo_b, wo_c, h,
                      m[:b, 2][:, None, :], m[b, 2][None, :], t_rows=t_rows, s_rows=s_rows)

        hn, aff = _moe_norm(h, norm2_g[l][None, :], lat(3, 4), cxm(3, 4), wr, b=b, t_rows=t_rows, s_rows=s_rows,
                            n_exp=n_exp)
        i_lat, g_lat = _topk(aff, tri, ones, row0=0, n=s_rows, k=cap_lat)
        i_ctx, g_ctx = _topk(aff, tri, ones, row0=s_rows, n=l_rows, k=cap_ctx)
        em = lambda a: jnp.transpose(a[:, :, :n_exp], (2, 0, 1))
        idx = jnp.concatenate([em(i_lat) + tok0, em(i_ctx) + tok0 + s_rows], axis=2)
        gates = jnp.concatenate([em(g_lat), em(g_ctx)], axis=2)
        h = _experts(idx[:, :, None, :], gates[..., None], m[:b, 5][:, None, :], m[b, 5][None, :],
                     ex_gate, ex_up, ex_down, l, hn, h, cap_lat=cap_lat)

    return h.reshape(b, t_rows, d)[:, :s_rows]
```

```python
import functools

import jax
import jax.numpy as jnp
from jax import lax
from jax.experimental import pallas as pl
from jax.experimental.pallas import tpu as pltpu

F32 = jnp.float32
BF16 = jnp.bfloat16
HIGHEST = lax.Precision.HIGHEST

GRID_W = 64
HEAD_DIM = 128
MLA_ROPE = 64
ROPE_F = MLA_ROPE // 4
ROPE_BASE = 10000.0
CAP_FACTOR = 2
EPS = 1e-6
N_MOD = 6
LOG2E = 1.4426950408889634
NEG = -1e30

V7X_LANES = 128
V7X_MXU = 256
V7X_VMEM_LIMIT = 56 * 1024 * 1024
NA_QROWS = 4
NA_KROWS = 12
VT_TILE = 256
FLASH_PAIRS_PER_TRIP = 2


def _pick(n, target, mult):
    best = None
    for d in range(mult, min(n, target) + 1, mult):
        if n % d == 0:
            best = d
    assert best is not None, (n, target, mult)
    return best


def _cparams(sem):
    return pltpu.CompilerParams(dimension_semantics=sem, vmem_limit_bytes=V7X_VMEM_LIMIT)


def _nt_dot(a, b):
    return lax.dot_general(a, b, (((1,), (1,)), ((), ())), preferred_element_type=F32)


def _rms(x, eps):
    return x * lax.rsqrt(jnp.mean(x * x, axis=-1, keepdims=True) + eps)


def _ada_kernel(cond_ref, down_ref, up_ref, bias_ref, out_ref, t_ref):
    @pl.when(pl.program_id(1) == 0)
    def _():
        c = cond_ref[...]
        s = c / (1.0 + jnp.exp(-c))
        t_ref[...] = jnp.dot(s, down_ref[0], precision=HIGHEST, preferred_element_type=F32)

    out_ref[0] = jnp.dot(t_ref[...], up_ref[0], precision=HIGHEST, preferred_element_type=F32) + bias_ref[0]


def _ada_mod(cond, down, up, bias):
    depth, d, rk = down.shape
    n = up.shape[-1]
    tn = _pick(n, 4096, V7X_LANES)
    rows = cond.shape[0]
    return pl.pallas_call(
        _ada_kernel,
        grid=(depth, n // tn),
        in_specs=[
            pl.BlockSpec((rows, d), lambda l, j: (0, 0)),
            pl.BlockSpec((1, d, rk), lambda l, j: (l, 0, 0)),
            pl.BlockSpec((1, rk, tn), lambda l, j: (l, 0, j)),
            pl.BlockSpec((1, 1, tn), lambda l, j: (l, 0, j)),
        ],
        out_specs=pl.BlockSpec((1, rows, tn), lambda l, j: (l, 0, j)),
        out_shape=jax.ShapeDtypeStruct((depth, rows, n), F32),
        scratch_shapes=[pltpu.VMEM((rows, rk), F32)],
        compiler_params=_cparams(("arbitrary", "arbitrary")),
        name="ada_mod",
    )(cond, down, up, bias.reshape(depth, 1, n))


def _modulated_norm(x, g, lat_shift, lat_scale, ctx_shift, ctx_scale, is_ctx, eps):
    y = _rms(x, eps) * g
    shift = jnp.where(is_ctx, ctx_shift, lat_shift)
    scale = jnp.where(is_ctx, ctx_scale, lat_scale)
    return y * (1.0 + scale) + shift


def _row_is_ctx(tile_in_batch, tm, row0, rows, s_rows):
    r = tile_in_batch * tm + row0 + lax.broadcasted_iota(jnp.int32, (rows, 1), 0)
    return r >= s_rows


def _proj_in_kernel(h_ref, g_ref, lat_ref, ctx_ref, w_ref, o_ref, nx_ref, *, tm, tpb, s_rows, chunk, eps):
    @pl.when(pl.program_id(1) == 0)
    def _():
        tile = pl.program_id(0) % tpb

        def body(c, carry):
            r0 = pl.multiple_of(c * chunk, chunk)
            x = h_ref[pl.ds(r0, chunk), :]
            is_ctx = _row_is_ctx(tile, tm, r0, chunk, s_rows)
            y = _modulated_norm(x, g_ref[...], lat_ref[0, 0:1, :], lat_ref[0, 1:2, :],
                                ctx_ref[0:1, :], ctx_ref[1:2, :], is_ctx, eps)
            nx_ref[pl.ds(r0, chunk), :] = y.astype(BF16)
            return carry

        lax.fori_loop(0, tm // chunk, body, 0)

    o_ref[...] = jnp.dot(nx_ref[...], w_ref[...], preferred_element_type=F32).astype(o_ref.dtype)


def _proj_in(h, g, lat_mod, ctx_mod, w, *, t_rows, s_rows):
    r, d = h.shape
    n = w.shape[1]
    tm = _pick(t_rows, 768, 128)
    tn = _pick(n, 512, V7X_LANES)
    tpb = t_rows // tm
    kern = functools.partial(_proj_in_kernel, tm=tm, tpb=tpb, s_rows=s_rows, chunk=_pick(tm, 128, 8), eps=EPS)
    return pl.pallas_call(
        kern,
        grid=(r // tm, n // tn),
        in_specs=[
            pl.BlockSpec((tm, d), lambda i, j: (i, 0)),
            pl.BlockSpec((1, d), lambda i, j: (0, 0)),
            pl.BlockSpec((1, 2, d), lambda i, j: (i // tpb, 0, 0)),
            pl.BlockSpec((2, d), lambda i, j: (0, 0)),
            pl.BlockSpec((d, tn), lambda i, j: (0, j)),
        ],
        out_specs=pl.BlockSpec((tm, tn), lambda i, j: (i, j)),
        out_shape=jax.ShapeDtypeStruct((r, n), BF16),
        scratch_shapes=[pltpu.VMEM((tm, d), BF16)],
        compiler_params=_cparams(("parallel", "arbitrary")),
        name="proj_in",
    )(h, g, lat_mod, ctx_mod, w)


def _rope(r, cos, sin):
    return r * cos + pltpu.roll(r, V7X_LANES // 2, 1) * sin


def _mla_prep_kernel(cq_ref, ckv_ref, kr_ref, gqa_ref, gkva_ref, wq_ref, wk_ref, wv_ref, gq_ref, gk_ref,
                     cos_ref, sin_ref, q_ref, k_ref, vt_ref, *, heads, qk_dim, scale, eps):
    hd, hp = HEAD_DIM, V7X_MXU
    cqn = (_rms(cq_ref[...].astype(F32), eps) * gqa_ref[...]).astype(BF16)
    ckvn = (_rms(ckv_ref[...].astype(F32), eps) * gkva_ref[...]).astype(BF16)
    kr = kr_ref[...].astype(F32)
    cos, sin = cos_ref[...], sin_ref[...]
    kr2 = jnp.sum(kr * kr, axis=-1, keepdims=True)
    vt = _nt_dot(wv_ref[...], ckvn).astype(vt_ref.dtype)
    for c in range(vt_ref.shape[1]):
        vt_ref[0, c] = vt[:, c * VT_TILE:(c + 1) * VT_TILE]
    for h in range(heads):
        a = jnp.dot(cqn, wq_ref[:, h * hp:(h + 1) * hp], preferred_element_type=F32)
        inv = lax.rsqrt(jnp.sum(a * a, axis=-1, keepdims=True) / qk_dim + eps)
        an = a * inv * gq_ref[...]
        q_ref[:, h * hp:h * hp + hd] = (an[:, :hd] * scale).astype(q_ref.dtype)
        q_ref[:, h * hp + hd:(h + 1) * hp] = (_rope(an[:, hd:], cos, sin) * scale).astype(q_ref.dtype)
        kn = jnp.dot(ckvn, wk_ref[:, h * hd:(h + 1) * hd], preferred_element_type=F32)
        invk = lax.rsqrt((jnp.sum(kn * kn, axis=-1, keepdims=True) + kr2) / qk_dim + eps)
        k_ref[:, h * hp:h * hp + hd] = (kn * invk * gk_ref[:, :hd]).astype(k_ref.dtype)
        k_ref[:, h * hp + hd:(h + 1) * hp] = _rope(kr * invk * gk_ref[:, hd:], cos, sin).astype(k_ref.dtype)


def _mla_prep(px, lay, gqa, gkva, wq, wk, wv, gq, gk, cos, sin, *, t_rows, heads, qk_dim, scale):
    r = px.shape[0]
    tm = VT_TILE
    tpb = t_rows // tm
    vpt = tm // VT_TILE
    qr, kvr = gqa.shape[1], gkva.shape[1]
    kern = functools.partial(_mla_prep_kernel, heads=heads, qk_dim=qk_dim, scale=scale, eps=EPS)
    full = lambda a: pl.BlockSpec(a.shape, lambda i: (0,) * a.ndim, pipeline_mode=pl.Buffered(1))
    return pl.pallas_call(
        kern,
        grid=(r // tm,),
        in_specs=[
            pl.BlockSpec((tm, qr), lambda i: (i, lay["cq"] // qr)),
            pl.BlockSpec((tm, kvr), lambda i: (i, lay["ckv"] // kvr)),
            pl.BlockSpec((tm, V7X_LANES), lambda i: (i, lay["kr"] // V7X_LANES)),
            full(gqa), full(gkva), full(wq), full(wk), full(wv), full(gq), full(gk),
            pl.BlockSpec((tm, V7X_LANES), lambda i: (i % tpb, 0)),
            pl.BlockSpec((tm, V7X_LANES), lambda i: (i % tpb, 0)),
        ],
        out_specs=[
            pl.BlockSpec((tm, heads * V7X_MXU), lambda i: (i, 0)),
            pl.BlockSpec((tm, heads * V7X_MXU), lambda i: (i, 0)),
            pl.BlockSpec((1, vpt, heads * HEAD_DIM, VT_TILE), lambda i: (i // tpb, i % tpb, 0, 0)),
        ],
        out_shape=[
            jax.ShapeDtypeStruct((r, heads * V7X_MXU), BF16),
            jax.ShapeDtypeStruct((r, heads * V7X_MXU), BF16),
            jax.ShapeDtypeStruct((r // t_rows, t_rows // VT_TILE, heads * HEAD_DIM, VT_TILE), BF16),
        ],
        compiler_params=_cparams(("parallel",)),
        name="mla_prep",
    )(px, px, px, gqa, gkva, wq, wk, wv, gq, gk, cos, sin)


def _flash_kernel(q_ref, k_ref, vt_ref, o_ref, s0_ref, s1_ref, mx0_ref, mx1_ref, m_ref, l_ref, acc_ref, *, tk, nkv):
    m_ref[...] = jnp.full(m_ref.shape, -jnp.inf, F32)
    l_ref[...] = jnp.zeros(l_ref.shape, F32)
    acc_ref[...] = jnp.zeros(acc_ref.shape, F32)
    vpt = tk // VT_TILE

    def scores(j, s_ref, mx_ref):
        off = pl.multiple_of(j * tk, tk)
        st = _nt_dot(k_ref[0, pl.ds(off, tk), :], q_ref[0])
        s_ref[...] = st
        mx_ref[...] = jnp.max(st, axis=0, keepdims=True)

    def accumulate(j, s_ref, mx_ref):
        m_prev = m_ref[...]
        m_new = jnp.maximum(m_prev, mx_ref[...])
        alpha = jnp.exp2(m_prev - m_new)
        p = jnp.exp2(s_ref[...] - m_new)
        l_ref[...] = alpha * l_ref[...] + jnp.sum(p, axis=0, keepdims=True)
        pb = p.astype(BF16)
        pv = jnp.dot(vt_ref[0, j * vpt], pb[0:VT_TILE], preferred_element_type=F32)
        for c in range(1, vpt):
            pv = pv + jnp.dot(vt_ref[0, j * vpt + c], pb[c * VT_TILE:(c + 1) * VT_TILE],
                              preferred_element_type=F32)
        acc_ref[...] = alpha * acc_ref[...] + pv
        m_ref[...] = m_new

    scores(0, s0_ref, mx0_ref)
    npairs = (nkv - 1) // 2

    def pair(j):
        scores(j + 1, s1_ref, mx1_ref)
        accumulate(j, s0_ref, mx0_ref)
        scores(j + 2, s0_ref, mx0_ref)
        accumulate(j + 1, s1_ref, mx1_ref)

    def body(jj, carry):
        for u in range(FLASH_PAIRS_PER_TRIP):
            pair(2 * (jj * FLASH_PAIRS_PER_TRIP + u))
        return carry

    ntrips = npairs // FLASH_PAIRS_PER_TRIP
    lax.fori_loop(0, ntrips, body, 0)
    for pr in range(ntrips * FLASH_PAIRS_PER_TRIP, npairs):
        pair(2 * pr)
    if (nkv - 1) % 2 == 0:
        accumulate(nkv - 1, s0_ref, mx0_ref)
    else:
        scores(nkv - 1, s1_ref, mx1_ref)
        accumulate(nkv - 2, s0_ref, mx0_ref)
        accumulate(nkv - 1, s1_ref, mx1_ref)
    o_ref[0] = (acc_ref[...] / l_ref[...]).T.astype(o_ref.dtype)


def _flash(q, k, vt, *, heads, q_row0, n_q, kv_row0, n_kv, tq, tk, name):
    b = q.shape[0]
    dk, dv = V7X_MXU, HEAD_DIM
    assert q_row0 % tq == 0 and n_q % tq == 0 and kv_row0 % n_kv == 0 and n_kv % tk == 0
    assert tk % VT_TILE == 0
    kern = functools.partial(_flash_kernel, tk=tk, nkv=n_kv // tk)
    return pl.pallas_call(
        kern,
        grid=(b, heads, n_q // tq),
        in_specs=[
            pl.BlockSpec((1, tq, dk), lambda bi, h, i: (bi, q_row0 // tq + i, h)),
            pl.BlockSpec((1, n_kv, dk), lambda bi, h, i: (bi, kv_row0 // n_kv, h)),
            pl.BlockSpec((1, n_kv // VT_TILE, dv, VT_TILE), lambda bi, h, i: (bi, kv_row0 // n_kv, h, 0)),
        ],
        out_specs=pl.BlockSpec((1, tq, dv), lambda bi, h, i: (bi, i, h)),
        out_shape=jax.ShapeDtypeStruct((b, n_q, heads * dv), BF16),
        scratch_shapes=[pltpu.VMEM((tk, tq), F32), pltpu.VMEM((tk, tq), F32),
                        pltpu.VMEM((1, tq), F32), pltpu.VMEM((1, tq), F32),
                        pltpu.VMEM((1, tq), F32), pltpu.VMEM((1, tq), F32), pltpu.VMEM((dv, tq), F32)],
        compiler_params=_cparams(("parallel", "parallel", "arbitrary")),
        name=name,
    )(q, k, vt)


def _na_table_kernel(rpb_ref, tbl_ref, *, win_r, win_c, w):
    h = pl.program_id(0)
    q = lax.broadcasted_iota(jnp.int32, (w, 2 * w), 0)
    lane = lax.broadcasted_iota(jnp.int32, (w, 2 * w), 1)
    c0 = jnp.clip(q - win_c // 2, 0, w - win_c)
    n_d = 2 * win_r - 1
    for side in range(2):
        kc = lane - side * w
        inside = (kc >= c0) & (kc < c0 + win_c)
        own_half = (kc >= 0) & (kc < w)
        rel = kc - q + (win_c - 1)
        for d in range(n_d):
            acc = jnp.full((w, 2 * w), NEG, F32)
            for dc in range(2 * win_c - 1):
                acc = jnp.where(inside & (rel == dc), rpb_ref[h, d, dc] * LOG2E, acc)
            tbl_ref[0, side, d] = jnp.where(own_half, acc, 0.0)
        tbl_ref[0, side, n_d] = jnp.where(own_half, NEG, 0.0)


def _na_table(rpb):
    heads, n_d, n_c = rpb.shape
    win_r, win_c = (n_d + 1) // 2, (n_c + 1) // 2
    kern = functools.partial(_na_table_kernel, win_r=win_r, win_c=win_c, w=GRID_W)
    return pl.pallas_call(
        kern,
        grid=(heads,),
        in_specs=[pl.BlockSpec(memory_space=pltpu.SMEM)],
        out_specs=pl.BlockSpec((1, 2, n_d + 1, GRID_W, 2 * GRID_W), lambda h: (h, 0, 0, 0, 0)),
        out_shape=jax.ShapeDtypeStruct((heads, 2, n_d + 1, GRID_W, 2 * GRID_W), F32),
        compiler_params=_cparams(("arbitrary",)),
        name="na_table",
    )(rpb)


def _na_kernel(q_ref, k_ref, v_ref, gq_ref, gk_ref, tbl_ref, o_ref, kn_ref, bias_ref, sl0_ref, sc0_ref,
               sl1_ref, sc1_ref, *, s_rows, l_rows, grid_rows, win_r, scale, eps, chunk):
    w = GRID_W
    t_rows = s_rows + l_rows
    nq, nk = NA_QROWS * w, NA_KROWS * w

    def kn_body(c, carry):
        r0 = pl.multiple_of(c * chunk, chunk)
        kk = k_ref[0, pl.ds(r0, chunk), :].astype(F32)
        kn_ref[pl.ds(r0, chunk), :] = (_rms(kk, eps) * gk_ref[...]).astype(BF16)
        return carry

    lax.fori_loop(0, t_rows // chunk, kn_body, 0)
    kc = kn_ref[s_rows:t_rows, :]
    vc = v_ref[0, s_rows:t_rows, :]

    def norm_q(x):
        return (_rms(x.astype(F32), eps) * gq_ref[...] * scale).astype(BF16)

    def softmax_pv(parts):
        m = functools.reduce(jnp.maximum, [jnp.max(s, axis=-1, keepdims=True) for s, _ in parts])
        ps = [jnp.exp2(s - m) for s, _ in parts]
        den = functools.reduce(lambda a, b: a + b, [jnp.sum(p, axis=-1, keepdims=True) for p in ps])
        num = functools.reduce(lambda a, b: a + b,
                               [jnp.dot(p.astype(BF16), vv, preferred_element_type=F32)
                                for p, (_, vv) in zip(ps, parts)])
        return num / den

    def key_start(i):
        return jnp.clip(i * NA_QROWS - win_r // 2, 0, grid_rows - NA_KROWS)

    def scores(i, sl_ref, sc_ref):
        q0 = pl.multiple_of(i * nq, nq)
        ks = key_start(i)
        k0 = pl.multiple_of(ks * w, w)
        qn = norm_q(q_ref[0, pl.ds(q0, nq), :])
        for qr in range(NA_QROWS):
            r = i * NA_QROWS + qr
            r0 = jnp.clip(r - win_r // 2, 0, grid_rows - win_r)
            for p in range(NA_KROWS // 2):
                ds = []
                for side in range(2):
                    kr = ks + 2 * p + side
                    ok = (kr >= r0) & (kr < r0 + win_r)
                    ds.append(jnp.where(ok, kr - r + (win_r - 1), 2 * win_r - 1))
                bias_ref[qr * w:(qr + 1) * w, p * 2 * w:(p + 1) * 2 * w] = (
                    tbl_ref[0, 0, ds[0]] + tbl_ref[0, 1, ds[1]])
        sl_ref[...] = _nt_dot(qn, kn_ref[pl.ds(k0, nk), :]) + bias_ref[...]
        sc_ref[...] = _nt_dot(qn, kc)

    def attend(i, sl_ref, sc_ref):
        q0 = pl.multiple_of(i * nq, nq)
        k0 = pl.multiple_of(key_start(i) * w, w)
        o = softmax_pv([(sc_ref[...], vc), (sl_ref[...], v_ref[0, pl.ds(k0, nk), :])])
        o_ref[0, pl.ds(q0, nq), :] = o.astype(o_ref.dtype)

    nblk = grid_rows // NA_QROWS
    scores(0, sl0_ref, sc0_ref)
    npairs = (nblk - 1) // 2

    def pair_body(jj, carry):
        i = 2 * jj
        scores(i + 1, sl1_ref, sc1_ref)
        attend(i, sl0_ref, sc0_ref)
        scores(i + 2, sl0_ref, sc0_ref)
        attend(i + 1, sl1_ref, sc1_ref)
        return carry

    lax.fori_loop(0, npairs, pair_body, 0)
    if (nblk - 1) % 2 == 0:
        attend(nblk - 1, sl0_ref, sc0_ref)
    else:
        scores(nblk - 1, sl1_ref, sc1_ref)
        attend(nblk - 2, sl0_ref, sc0_ref)
        attend(nblk - 1, sl1_ref, sc1_ref)

    qc = norm_q(q_ref[0, s_rows:t_rows, :])
    o_ref[0, s_rows:t_rows, :] = softmax_pv([(_nt_dot(qc, kc), vc)]).astype(o_ref.dtype)


def _na_attention(px3, lay, gq, gk, tbl, *, heads, s_rows, l_rows, win_r, scale):
    b, t, _ = px3.shape
    hd = HEAD_DIM
    grid_rows = s_rows // GRID_W
    assert 2 * GRID_W == V7X_LANES and grid_rows % NA_QROWS == 0 and grid_rows >= NA_KROWS
    assert NA_KROWS >= NA_QROWS + win_r - 1 and NA_KROWS % 2 == 0
    kern = functools.partial(_na_kernel, s_rows=s_rows, l_rows=l_rows, grid_rows=grid_rows, win_r=win_r,
                             scale=scale, eps=EPS, chunk=_pick(t, 256, 16))
    col = lambda name: (lambda bi, h: (bi, 0, lay[name] // hd + h))
    return pl.pallas_call(
        kern,
        grid=(b, heads),
        in_specs=[
            pl.BlockSpec((1, t, hd), col("nq")),
            pl.BlockSpec((1, t, hd), col("nk")),
            pl.BlockSpec((1, t, hd), col("nv")),
            pl.BlockSpec((1, hd), lambda bi, h: (0, 0)),
            pl.BlockSpec((1, hd), lambda bi, h: (0, 0)),
            pl.BlockSpec((1,) + tbl.shape[1:], lambda bi, h: (h, 0, 0, 0, 0)),
        ],
        out_specs=pl.BlockSpec((1, t, hd), lambda bi, h: (bi, 0, h)),
        out_shape=jax.ShapeDtypeStruct((b, t, heads * hd), BF16),
        scratch_shapes=[pltpu.VMEM((t, hd), BF16), pltpu.VMEM((NA_QROWS * GRID_W, NA_KROWS * GRID_W), F32),
                        pltpu.VMEM((NA_QROWS * GRID_W, NA_KROWS * GRID_W), F32),
                        pltpu.VMEM((NA_QROWS * GRID_W, l_rows), F32),
                        pltpu.VMEM((NA_QROWS * GRID_W, NA_KROWS * GRID_W), F32),
                        pltpu.VMEM((NA_QROWS * GRID_W, l_rows), F32)],
        compiler_params=_cparams(("parallel", "parallel")),
        name="na_attn",
    )(px3, px3, px3, gq, gk, tbl)


def _conv_kernel(cb_ref, cc_ref, cx_ref, ccp_ref, cxp_ref, ccn_ref, cxn_ref, w_ref, o_ref, *,
                 tm, tpb, s_rows, t_rows, halo):
    u = cc_ref[...].astype(F32) * cx_ref[...].astype(F32)
    u_before = ccp_ref[halo - 1:halo, :].astype(F32) * cxp_ref[halo - 1:halo, :].astype(F32)
    u_after = ccn_ref[0:1, :].astype(F32) * cxn_ref[0:1, :].astype(F32)
    loc = lax.broadcasted_iota(jnp.int32, (tm, 1), 0)
    row = (pl.program_id(0) % tpb) * tm + loc
    first = (row == 0) | (row == s_rows)
    last = (row == s_rows - 1) | (row == t_rows - 1)
    u_prev = jnp.where(loc == 0, u_before, pltpu.roll(u, 1, 0))
    u_prev = jnp.where(first, 0.0, u_prev)
    u_next = jnp.where(loc == tm - 1, u_after, pltpu.roll(u, tm - 1, 0))
    u_next = jnp.where(last, 0.0, u_next)
    y = w_ref[0:1, :] * u_prev + w_ref[1:2, :] * u + w_ref[2:3, :] * u_next
    o_ref[...] = (cb_ref[...].astype(F32) * y).astype(o_ref.dtype)


def _gated_conv(px, lay, w, *, t_rows, s_rows):
    r = px.shape[0]
    c = w.shape[1]
    halo = 16
    tm = _pick(t_rows, 256, 128)
    tpb = t_rows // tm
    nb = r // halo
    kern = functools.partial(_conv_kernel, tm=tm, tpb=tpb, s_rows=s_rows, t_rows=t_rows, halo=halo)
    blk = lambda name: pl.BlockSpec((tm, c), lambda i: (i, lay[name] // c))
    prev = lambda name: pl.BlockSpec((halo, c), lambda i: (jnp.maximum(i * (tm // halo) - 1, 0), lay[name] // c))
    nxt = lambda name: pl.BlockSpec((halo, c), lambda i: (jnp.minimum((i + 1) * (tm // halo), nb - 1), lay[name] // c))
    return pl.pallas_call(
        kern,
        grid=(r // tm,),
        in_specs=[blk("cb"), blk("cc"), blk("cx"), prev("cc"), prev("cx"), nxt("cc"), nxt("cx"),
                  pl.BlockSpec(w.shape, lambda i: (0, 0))],
        out_specs=pl.BlockSpec((tm, c), lambda i: (i, 0)),
        out_shape=jax.ShapeDtypeStruct((r, c), BF16),
        compiler_params=_cparams(("parallel",)),
        name="gated_conv",
    )(px, px, px, px, px, px, px, w)


def _proj_out_kernel(oa_ref, ob_ref, oc_ref, wa_ref, wb_ref, wc_ref, h_ref, lat_ref, ctx_ref, o_ref, *,
                     tm, tpb, s_rows):
    y = jnp.dot(oa_ref[...], wa_ref[...], preferred_element_type=F32)
    y = y + jnp.dot(ob_ref[...], wb_ref[...], preferred_element_type=F32)
    y = y + jnp.dot(oc_ref[...], wc_ref[...], preferred_element_type=F32)
    is_ctx = _row_is_ctx(pl.program_id(0) % tpb, tm, 0, tm, s_rows)
    gate = jnp.where(is_ctx, ctx_ref[...], lat_ref[0])
    o_ref[...] = h_ref[...] + gate * y


def _proj_out(oa, ob, oc, wa, wb, wc, h, lat_gate, ctx_gate, *, t_rows, s_rows):
    r, d = h.shape
    tm = _pick(t_rows, 768, 128)
    tn = _pick(d, 1024, V7X_LANES)
    tpb = t_rows // tm
    kern = functools.partial(_proj_out_kernel, tm=tm, tpb=tpb, s_rows=s_rows)
    lhs = lambda a: pl.BlockSpec((tm, a.shape[1]), lambda i, j: (i, 0))
    rhs = lambda a: pl.BlockSpec((a.shape[0], tn), lambda i, j: (0, j))
    return pl.pallas_call(
        kern,
        grid=(r // tm, d // tn),
        in_specs=[lhs(oa), lhs(ob), lhs(oc), rhs(wa), rhs(wb), rhs(wc),
                  pl.BlockSpec((tm, tn), lambda i, j: (i, j)),
                  pl.BlockSpec((1, 1, tn), lambda i, j: (i // tpb, 0, j)),
                  pl.BlockSpec((1, tn), lambda i, j: (0, j))],
        out_specs=pl.BlockSpec((tm, tn), lambda i, j: (i, j)),
        out_shape=jax.ShapeDtypeStruct((r, d), F32),
        input_output_aliases={6: 0},
        compiler_params=_cparams(("parallel", "arbitrary")),
        name="proj_out",
    )(oa, ob, oc, wa, wb, wc, h, lat_gate, ctx_gate)


def _moe_norm_kernel(h_ref, g_ref, lat_ref, ctx_ref, wr_ref, hn_ref, aff_ref, *, tm, tpb, s_rows, n_exp, eps):
    is_ctx = _row_is_ctx(pl.program_id(0) % tpb, tm, 0, tm, s_rows)
    hn = _modulated_norm(h_ref[...], g_ref[...], lat_ref[0, 0:1, :], lat_ref[0, 1:2, :],
                         ctx_ref[0:1, :], ctx_ref[1:2, :], is_ctx, eps)
    hb = hn.astype(BF16)
    half = hn.shape[1] // 2
    hf = hb.astype(F32)
    hn_ref[...] = pltpu.bitcast(hf[:, :half], jnp.uint32) | (pltpu.bitcast(hf[:, half:], jnp.uint32) >> 16)
    logits = jnp.dot(hb, wr_ref[...], preferred_element_type=F32)
    lane = lax.broadcasted_iota(jnp.int32, logits.shape, 1)
    logits = jnp.where(lane < n_exp, logits, -jnp.inf)
    e = jnp.exp(logits - jnp.max(logits, axis=-1, keepdims=True))
    aff = e / jnp.sum(e, axis=-1, keepdims=True)
    aff_ref[0] = aff.T[:n_exp, :]


def _moe_norm(h, g, lat_mod, ctx_mod, wr, *, b, t_rows, s_rows, n_exp):
    r, d = h.shape
    tm = _pick(t_rows, 256, 128)
    tpb = t_rows // tm
    kern = functools.partial(_moe_norm_kernel, tm=tm, tpb=tpb, s_rows=s_rows, n_exp=n_exp, eps=EPS)
    return pl.pallas_call(
        kern,
        grid=(r // tm,),
        in_specs=[
            pl.BlockSpec((tm, d), lambda i: (i, 0)),
            pl.BlockSpec((1, d), lambda i: (0, 0)),
            pl.BlockSpec((1, 2, d), lambda i: (i // tpb, 0, 0)),
            pl.BlockSpec((2, d), lambda i: (0, 0)),
            pl.BlockSpec(wr.shape, lambda i: (0, 0)),
        ],
        out_specs=[pl.BlockSpec((tm, d // 2), lambda i: (i, 0)),
                   pl.BlockSpec((1, n_exp, tm), lambda i: (i // tpb, 0, i % tpb))],
        out_shape=[jax.ShapeDtypeStruct((r, d // 2), jnp.uint32), jax.ShapeDtypeStruct((b, n_exp, t_rows), F32)],
        compiler_params=_cparams(("parallel",)),
        name="moe_norm",
    )(h, g, lat_mod, ctx_mod, wr)


def _topk_kernel(aff_ref, tri_ref, ones_ref, idx_ref, gate_ref, posm_ref, *, n, k, n_exp):
    ln = V7X_LANES
    nt = n // ln
    a = aff_ref[0]
    bits = pltpu.bitcast(a, jnp.int32)
    thr = jnp.zeros((n_exp, 1), jnp.int32)
    for bit in range(30, -1, -1):
        cand = thr | (1 << bit)
        cnt = jnp.sum((bits >= cand).astype(jnp.int32), axis=1, keepdims=True)
        thr = jnp.where(cnt >= k, cand, thr)
    gt = bits > thr
    eq = bits == thr

    def excl_prefix(x):
        xb = x.astype(BF16)
        run = jnp.zeros((n_exp, ln), F32)
        outs = []
        for t in range(nt):
            blk = xb[:, t * ln:(t + 1) * ln]
            outs.append(jnp.dot(blk, tri_ref[...], preferred_element_type=F32) + run)
            run = run + jnp.dot(blk, ones_ref[...], preferred_element_type=F32)
        return jnp.concatenate(outs, axis=1)

    need = (k - jnp.sum(gt.astype(jnp.int32), axis=1, keepdims=True)).astype(F32)
    eqf = jnp.where(eq, 1.0, 0.0)
    sel = gt | (eq & (excl_prefix(eqf) < need))
    pos = excl_prefix(jnp.where(sel, 1.0, 0.0))
    posm_ref[...] = jnp.where(sel, pos, -1.0)

    lane = lax.broadcasted_iota(jnp.int32, (8, ln), 1)
    lane_f = lane.astype(F32)

    def jb_body(jb, carry):
        slot = (jb * 8 + lax.broadcasted_iota(jnp.int32, (8, 1), 0)).astype(F32)
        out_i = jnp.zeros((8, ln), F32)
        out_g = jnp.zeros((8, ln), F32)
        for e in range(n_exp):
            acc_i = jnp.zeros((8, ln), F32)
            acc_g = jnp.zeros((8, ln), F32)
            for t in range(nt):
                hit = posm_ref[e:e + 1, t * ln:(t + 1) * ln] == slot
                acc_i = acc_i + jnp.where(hit, lane_f + float(t * ln), 0.0)
                acc_g = acc_g + jnp.where(hit, aff_ref[0, e:e + 1, t * ln:(t + 1) * ln], 0.0)
            out_i = jnp.where(lane == e, jnp.sum(acc_i, axis=1, keepdims=True), out_i)
            out_g = jnp.where(lane == e, jnp.sum(acc_g, axis=1, keepdims=True), out_g)
        r0 = pl.multiple_of(jb * 8, 8)
        idx_ref[0, pl.ds(r0, 8), :] = out_i.astype(jnp.int32)
        gate_ref[0, pl.ds(r0, 8), :] = out_g
        return carry

    lax.fori_loop(0, k // 8, jb_body, 0)


def _topk(aff, tri, ones, *, row0, n, k):
    b, n_exp, _ = aff.shape
    assert row0 % n == 0 and n % V7X_LANES == 0 and k % 8 == 0 and n_exp <= V7X_LANES
    kern = functools.partial(_topk_kernel, n=n, k=k, n_exp=n_exp)
    return pl.pallas_call(
        kern,
        grid=(b,),
        in_specs=[pl.BlockSpec((1, n_exp, n), lambda bi: (bi, 0, row0 // n)),
                  pl.BlockSpec(tri.shape, lambda bi: (0, 0)),
                  pl.BlockSpec(ones.shape, lambda bi: (0, 0))],
        out_specs=[pl.BlockSpec((1, k, V7X_LANES), lambda bi: (bi, 0, 0)),
                   pl.BlockSpec((1, k, V7X_LANES), lambda bi: (bi, 0, 0))],
        out_shape=[jax.ShapeDtypeStruct((b, k, V7X_LANES), jnp.int32),
                   jax.ShapeDtypeStruct((b, k, V7X_LANES), F32)],
        scratch_shapes=[pltpu.VMEM((n_exp, n), F32)],
        compiler_params=_cparams(("parallel",)),
        name="topk",
    )(aff, tri, ones)


ROW_UNROLL = 16
SEM_HN, SEM_H, SEM_OUT = 0, 2, 3


def _expert_kernel(idx_ref, g_ref, lat_ref, ctx_ref, wg_ref, wu_ref, wd_ref, hn_hbm, h_hbm, out_hbm,
                   xe_ref, y_ref, st_ref, sem, *, rows, cap_lat, dcols, nchunk):
    del h_hbm
    e, bi, f = pl.program_id(0), pl.program_id(1), pl.program_id(2)
    ne, nb, nf = pl.num_programs(0), pl.num_programs(1), pl.num_programs(2)
    ch = rows // nchunk
    half = xe_ref.shape[1] // 2

    def issue_rows(n_rows, make):
        def trip(jj, carry):
            base = pl.multiple_of(jj * ROW_UNROLL, ROW_UNROLL)
            for u in range(ROW_UNROLL):
                make(base, u).start()
            return carry

        lax.fori_loop(0, n_rows // ROW_UNROLL, trip, 0)

    def hn_row(c, base, u):
        return pltpu.make_async_copy(hn_hbm.at[pl.ds(idx_ref[0, 0, 0, c * ch + base + u], 1)],
                                     st_ref.at[c % 2, pl.ds(base + u, 1)], sem.at[SEM_HN + c % 2])

    def hn_chunk_all(c):
        return pltpu.make_async_copy(hn_hbm.at[pl.ds(0, ch)], st_ref.at[c % 2], sem.at[SEM_HN + c % 2])

    def h_row(base, u):
        return pltpu.make_async_copy(out_hbm.at[pl.ds(idx_ref[0, 0, 0, base + u], 1)],
                                     y_ref.at[pl.ds(base + u, 1)], sem.at[SEM_H])

    def out_row(base, u):
        return pltpu.make_async_copy(y_ref.at[pl.ds(base + u, 1)],
                                     out_hbm.at[pl.ds(idx_ref[0, 0, 0, base + u], 1)], sem.at[SEM_OUT])

    def all_rows(s):
        return pltpu.make_async_copy(out_hbm.at[pl.ds(0, rows)], y_ref, sem.at[s])

    @pl.when(f == 0)
    def _():
        issue_rows(ch, functools.partial(hn_row, 0))

        @pl.when((e > 0) | (bi > 0))
        def _():
            all_rows(SEM_OUT).wait()

        issue_rows(rows, h_row)
        for c in range(nchunk):
            if c + 1 < nchunk:
                issue_rows(ch, functools.partial(hn_row, c + 1))
            hn_chunk_all(c).wait()
            p = st_ref[c % 2]
            hi = pltpu.bitcast(p & jnp.uint32(0xFFFF0000), F32)
            lo = pltpu.bitcast(p << 16, F32)
            xe_ref[c * ch:(c + 1) * ch, :half] = hi.astype(BF16)
            xe_ref[c * ch:(c + 1) * ch, half:] = lo.astype(BF16)

    xe = xe_ref[...]
    a = jnp.dot(xe, wg_ref[0, 0].astype(BF16), preferred_element_type=F32)
    u = jnp.dot(xe, wu_ref[0, 0].astype(BF16), preferred_element_type=F32)
    hm = ((a / (1.0 + jnp.exp(-a))) * u).astype(BF16)

    @pl.when(f == 0)
    def _():
        all_rows(SEM_H).wait()

    g = g_ref[0, 0]
    d = y_ref.shape[1]
    for c in range(d // dcols):
        cs = slice(c * dcols, (c + 1) * dcols)
        y = jnp.dot(hm, wd_ref[0, 0, :, cs].astype(BF16), preferred_element_type=F32) * g
        y_ref[0:cap_lat, cs] = y_ref[0:cap_lat, cs] + lat_ref[0, :, cs] * y[0:cap_lat]
        y_ref[cap_lat:rows, cs] = y_ref[cap_lat:rows, cs] + ctx_ref[:, cs] * y[cap_lat:rows]

    @pl.when(f == nf - 1)
    def _():
        issue_rows(rows, out_row)

        @pl.when((e == ne - 1) & (bi == nb - 1))
        def _():
            all_rows(SEM_OUT).wait()


def _experts(idx, gates, lat_gate, ctx_gate, wg, wu, wd, layer, hn, h, *, cap_lat):
    n_exp, b, _, rows = idx.shape
    d, ff = wg.shape[2], wg.shape[3]
    fc = _pick(ff, 256, V7X_LANES)
    nchunk = max(n for n in range(1, 7) if rows % (16 * n) == 0)
    assert rows % 16 == 0 and hn.shape[1] * 2 == d
    kern = functools.partial(_expert_kernel, rows=rows, cap_lat=cap_lat, dcols=_pick(d, 1024, V7X_LANES),
                             nchunk=nchunk)
    assert (rows // nchunk) % ROW_UNROLL == 0
    return pl.pallas_call(
        kern,
        grid=(n_exp, b, ff // fc),
        in_specs=[
            pl.BlockSpec((1, 1, 1, rows), lambda e, bi, f: (e, bi, 0, 0), memory_space=pltpu.SMEM),
            pl.BlockSpec((1, 1, rows, 1), lambda e, bi, f: (e, bi, 0, 0)),
            pl.BlockSpec((1, 1, d), lambda e, bi, f: (bi, 0, 0)),
            pl.BlockSpec((1, d), lambda e, bi, f: (0, 0)),
            pl.BlockSpec((1, 1, d, fc), lambda e, bi, f: (layer, e, 0, f)),
            pl.BlockSpec((1, 1, d, fc), lambda e, bi, f: (layer, e, 0, f)),
            pl.BlockSpec((1, 1, fc, d), lambda e, bi, f: (layer, e, f, 0)),
            pl.BlockSpec(memory_space=pl.ANY),
            pl.BlockSpec(memory_space=pl.ANY),
        ],
        out_specs=pl.BlockSpec(memory_space=pl.ANY),
        out_shape=jax.ShapeDtypeStruct(h.shape, F32),
        scratch_shapes=[pltpu.VMEM((rows, d), BF16), pltpu.VMEM((rows, d), F32),
                        pltpu.VMEM((2, rows // nchunk, d // 2), jnp.uint32), pltpu.SemaphoreType.DMA((4,))],
        input_output_aliases={8: 0},
        compiler_params=_cparams(("arbitrary", "arbitrary", "arbitrary")),
        name="experts",
    )(idx, gates, lat_gate, ctx_gate, wg, wu, wd, hn, h)


def _in_layout(conv_ch, q_rank, kv_rank, na_cols):
    lay, off = {}, 0

    def put(name, width, align):
        nonlocal off
        off = -(-off // align) * align
        lay[name] = off
        off += width

    for name in ("cb", "cc", "cx"):
        put(name, conv_ch, conv_ch)
    put("cq", q_rank, q_rank)
    for name in ("nq", "nk", "nv"):
        put(name, na_cols, HEAD_DIM)
    put("ckv", kv_rank, kv_rank)
    put("kr", V7X_LANES, V7X_LANES)
    lay["n"] = -(-off // 512) * 512
    return lay


def _spread_rope(a):
    half = MLA_ROPE // 2
    z = jnp.zeros(a.shape[:-1] + (V7X_LANES // 2 - half,), a.dtype)
    return jnp.concatenate([a[..., :half], z, a[..., half:], z], axis=-1)


def _pad_heads(a, heads, nope):
    a = a.reshape(a.shape[:-1] + (heads, nope + MLA_ROPE))
    a = jnp.concatenate([a[..., :nope], _spread_rope(a[..., nope:])], axis=-1)
    return a.reshape(a.shape[:-2] + (heads * V7X_MXU,))


def _rope_tables(s_rows, l_rows):
    pos = jnp.arange(s_rows, dtype=jnp.int32)
    row = (pos // GRID_W).astype(F32)
    col = (pos % GRID_W).astype(F32)
    inv = jnp.power(ROPE_BASE, -jnp.arange(ROPE_F, dtype=F32) / ROPE_F)
    ang = jnp.concatenate([row[:, None] * inv, col[:, None] * inv], axis=1)
    cos = jnp.concatenate([jnp.cos(ang), jnp.ones((l_rows, 2 * ROPE_F), F32)], axis=0)
    sin = jnp.concatenate([jnp.sin(ang), jnp.zeros((l_rows, 2 * ROPE_F), F32)], axis=0)
    return _spread_rope(jnp.concatenate([cos, cos], axis=1)), _spread_rope(jnp.concatenate([-sin, sin], axis=1))


def kernel(x, c, ctx, c_ctx, ada_down, ada_up, ada_bias, norm1_g, w_in, mla_qa_g, mla_kva_g, mla_w_uq, mla_w_ukv, mla_q_g, mla_k_g, conv_w, na_q_g, na_k_g, na_rpb, w_out, norm2_g, w_router, ex_gate, ex_up, ex_down):
    b, s_rows, d = x.shape
    l_rows = ctx.shape[1]
    t_rows = s_rows + l_rows
    depth = w_in.shape[0]
    q_rank, kv_rank = mla_qa_g.shape[1], mla_kva_g.shape[1]
    qk_dim = mla_q_g.shape[1]
    nope = qk_dim - MLA_ROPE
    assert nope == HEAD_DIM
    mla_heads = mla_w_uq.shape[2] // qk_dim
    conv_ch = conv_w.shape[2]
    na_heads = na_rpb.shape[1]
    na_cols = na_heads * HEAD_DIM
    win_r = (na_rpb.shape[2] + 1) // 2
    n_exp = w_router.shape[2]
    cap_lat = CAP_FACTOR * s_rows // n_exp
    cap_ctx = CAP_FACTOR * l_rows // n_exp
    lay = _in_layout(conv_ch, q_rank, kv_rank, na_cols)
    mla_cols = mla_heads * HEAD_DIM

    o_cq = 0
    o_ckv = o_cq + q_rank
    o_kr = o_ckv + kv_rank
    o_cb = o_kr + MLA_ROPE
    o_cc = o_cb + conv_ch
    o_cx = o_cc + conv_ch
    o_nq = o_cx + conv_ch
    o_nk = o_nq + na_cols
    o_nv = o_nk + na_cols

    n_cond = -(-(b + 1) // 8) * 8
    cond = jnp.zeros((n_cond, d), F32).at[:b].set(c).at[b].set(c_ctx)
    mods = _ada_mod(cond, ada_down, ada_up, ada_bias).reshape(depth, n_cond, N_MOD, d)

    cos_t, sin_t = _rope_tables(s_rows, l_rows)
    tri = jnp.triu(jnp.ones((V7X_LANES, V7X_LANES), BF16), 1)
    ones = jnp.ones((V7X_LANES, V7X_LANES), BF16)
    tok0 = (jnp.arange(b, dtype=jnp.int32) * t_rows)[None, :, None]

    h = jnp.concatenate([x, ctx], axis=1).reshape(b * t_rows, d)

    for l in range(depth):
        m = mods[l]
        lat = lambda i, j: jnp.stack([m[:b, i], m[:b, j]], axis=1)
        cxm = lambda i, j: jnp.stack([m[b, i], m[b, j]], axis=0)

        wl = w_in[l]
        sec ={"cb": wl[:, o_cb:o_cc], "cc": wl[:, o_cc:o_cx], "cx": wl[:, o_cx:o_nq], "cq": wl[:, o_cq:o_ckv],
               "nq": wl[:, o_nq:o_nk], "nk": wl[:, o_nk:o_nv], "nv": wl[:, o_nv:o_nv + na_cols],
               "ckv": wl[:, o_ckv:o_kr], "kr": _spread_rope(wl[:, o_kr:o_cb])}
        w_p = jnp.zeros((d, lay["n"]), BF16)
        for name, blk in sec.items():
            w_p = lax.dynamic_update_slice(w_p, blk.astype(BF16), (0, lay[name]))
        wq = _pad_heads(mla_w_uq[l], mla_heads, nope).astype(BF16)
        wkv = mla_w_ukv[l].reshape(kv_rank, mla_heads, 2 * HEAD_DIM)
        wk = wkv[:, :, :HEAD_DIM].reshape(kv_rank, mla_cols).astype(BF16)
        wv = wkv[:, :, HEAD_DIM:].reshape(kv_rank, mla_cols).T.astype(BF16)
        gq = _pad_heads(mla_q_g[l][None, :], 1, nope)
        gk = _pad_heads(mla_k_g[l][None, :], 1, nope)
        wo = w_out[l].astype(BF16)
        wo_a, wo_b, wo_c = wo[:mla_cols], wo[mla_cols:mla_cols + conv_ch], wo[mla_cols + conv_ch:]
        wr = jnp.zeros((d, V7X_LANES), BF16).at[:, :n_exp].set(w_router[l].astype(BF16))

        px = _proj_in(h, norm1_g[l][None, :], lat(0, 1), cxm(0, 1), w_p, t_rows=t_rows, s_rows=s_rows)
        q, k, vt = _mla_prep(px, lay, mla_qa_g[l][None, :], mla_kva_g[l][None, :], wq, wk, wv, gq, gk, cos_t, sin_t,
                            t_rows=t_rows, heads=mla_heads, qk_dim=qk_dim, scale=qk_dim ** -0.5 * LOG2E)
        q3, k3 = q.reshape(b, t_rows, -1), k.reshape(b, t_rows, -1)
        o_a = jnp.concatenate([
            _flash(q3, k3, vt, heads=mla_heads, q_row0=0, n_q=s_rows, kv_row0=0, n_kv=t_rows,
                   tq=_pick(s_rows, 2048, 256), tk=_pick(t_rows, 768, VT_TILE), name="mla_attn"),
            _flash(q3, k3, vt, heads=mla_heads, q_row0=s_rows, n_q=l_rows, kv_row0=s_rows, n_kv=l_rows,
                   tq=l_rows, tk=l_rows, name="mla_attn_ctx")], axis=1)
        o_b = _gated_conv(px, lay, conv_w[l], t_rows=t_rows, s_rows=s_rows)
        o_c = _na_attention(px.reshape(b, t_rows, -1), lay, na_q_g[l][None, :], na_k_g[l][None, :],
                            _na_table(na_rpb[l]), heads=na_heads, s_rows=s_rows, l_rows=l_rows, win_r=win_r,
                            scale=HEAD_DIM ** -0.5 * LOG2E)
        h = _proj_out(o_a.reshape(b * t_rows, -1), o_b, o_c.reshape(b * t_rows, -1), wo_a, wo_b, wo_c, h,
                      m[:b, 2][:, None, :], m[b, 2][None, :], t_rows=t_rows, s_rows=s_rows)

        hn, aff = _moe_norm(h, norm2_g[l][None, :], lat(3, 4), cxm(3, 4), wr, b=b, t_rows=t_rows, s_rows=s_rows,
                            n_exp=n_exp)
        i_lat, g_lat = _topk(aff, tri, ones, row0=0, n=s_rows, k=cap_lat)
        i_ctx, g_ctx = _topk(aff, tri, ones, row0=s_rows, n=l_rows, k=cap_ctx)
        em = lambda a: jnp.transpose(a[:, :, :n_exp], (2, 0, 1))
        idx = jnp.concatenate([em(i_lat) + tok0, em(i_ctx) + tok0 + s_rows], axis=2)
        gates = jnp.concatenate([em(g_lat), em(g_ctx)], axis=2)
        h = _experts(idx[:, :, None, :], gates[..., None], m[:b, 5][:, None, :], m[b, 5][None, :],
                     ex_gate, ex_up, ex_down, l, hn, h, cap_lat=cap_lat)

    return h.reshape(b, t_rows, d)[:, :s_rows]
```

```python
import functools

import jax
import jax.numpy as jnp
from jax import lax
from jax.experimental import pallas as pl
from jax.experimental.pallas import tpu as pltpu

F32 = jnp.float32
BF16 = jnp.bfloat16
HIGHEST = lax.Precision.HIGHEST

GRID_W = 64
HEAD_DIM = 128
MLA_ROPE = 64
ROPE_F = MLA_ROPE // 4
ROPE_BASE = 10000.0
CAP_FACTOR = 2
EPS = 1e-6
N_MOD = 6
LOG2E = 1.4426950408889634
NEG = -1e30

V7X_LANES = 128
V7X_MXU = 256
V7X_VMEM_LIMIT = 56 * 1024 * 1024
NA_QROWS = 4
NA_KROWS = 12
VT_TILE = 256
FLASH_PAIRS_PER_TRIP = 2


def _pick(n, target, mult):
    best = None
    for d in range(mult, min(n, target) + 1, mult):
        if n % d == 0:
            best = d
    assert best is not None, (n, target, mult)
    return best


def _cparams(sem):
    return pltpu.CompilerParams(dimension_semantics=sem, vmem_limit_bytes=V7X_VMEM_LIMIT)


def _nt_dot(a, b):
    return lax.dot_general(a, b, (((1,), (1,)), ((), ())), preferred_element_type=F32)


def _rms(x, eps):
    return x * lax.rsqrt(jnp.mean(x * x, axis=-1, keepdims=True) + eps)


def _ada_kernel(cond_ref, down_ref, up_ref, bias_ref, out_ref, t_ref):
    @pl.when(pl.program_id(1) == 0)
    def _():
        c = cond_ref[...]
        s = c / (1.0 + jnp.exp(-c))
        t_ref[...] = jnp.dot(s, down_ref[0], precision=HIGHEST, preferred_element_type=F32)

    out_ref[0] = jnp.dot(t_ref[...], up_ref[0], precision=HIGHEST, preferred_element_type=F32) + bias_ref[0]


def _ada_mod(cond, down, up, bias):
    depth, d, rk = down.shape
    n = up.shape[-1]
    tn = _pick(n, 4096, V7X_LANES)
    rows = cond.shape[0]
    return pl.pallas_call(
        _ada_kernel,
        grid=(depth, n // tn),
        in_specs=[
            pl.BlockSpec((rows, d), lambda l, j: (0, 0)),
            pl.BlockSpec((1, d, rk), lambda l, j: (l, 0, 0)),
            pl.BlockSpec((1, rk, tn), lambda l, j: (l, 0, j)),
            pl.BlockSpec((1, 1, tn), lambda l, j: (l, 0, j)),
        ],
        out_specs=pl.BlockSpec((1, rows, tn), lambda l, j: (l, 0, j)),
        out_shape=jax.ShapeDtypeStruct((depth, rows, n), F32),
        scratch_shapes=[pltpu.VMEM((rows, rk), F32)],
        compiler_params=_cparams(("arbitrary", "arbitrary")),
        name="ada_mod",
    )(cond, down, up, bias.reshape(depth, 1, n))


def _modulated_norm(x, g, lat_shift, lat_scale, ctx_shift, ctx_scale, is_ctx, eps):
    y = _rms(x, eps) * g
    shift = jnp.where(is_ctx, ctx_shift, lat_shift)
    scale = jnp.where(is_ctx, ctx_scale, lat_scale)
    return y * (1.0 + scale) + shift


def _row_is_ctx(tile_in_batch, tm, row0, rows, s_rows):
    r = tile_in_batch * tm + row0 + lax.broadcasted_iota(jnp.int32, (rows, 1), 0)
    return r >= s_rows


def _proj_in_kernel(h_ref, g_ref, lat_ref, ctx_ref, w_ref, o_ref, nx_ref, *, tm, tpb, s_rows, chunk, eps):
    @pl.when(pl.program_id(1) == 0)
    def _():
        tile = pl.program_id(0) % tpb

        def body(c, carry):
            r0 = pl.multiple_of(c * chunk, chunk)
            x = h_ref[pl.ds(r0, chunk), :]
            is_ctx = _row_is_ctx(tile, tm, r0, chunk, s_rows)
            y = _modulated_norm(x, g_ref[...], lat_ref[0, 0:1, :], lat_ref[0, 1:2, :],
                                ctx_ref[0:1, :], ctx_ref[1:2, :], is_ctx, eps)
            nx_ref[pl.ds(r0, chunk), :] = y.astype(BF16)
            return carry

        lax.fori_loop(0, tm // chunk, body, 0)

    o_ref[...] = jnp.dot(nx_ref[...], w_ref[...], preferred_element_type=F32).astype(o_ref.dtype)


def _proj_in(h, g, lat_mod, ctx_mod, w, *, t_rows, s_rows):
    r, d = h.shape
    n = w.shape[1]
    tm = _pick(t_rows, 768, 128)
    tn = _pick(n, 512, V7X_LANES)
    tpb = t_rows // tm
    kern = functools.partial(_proj_in_kernel, tm=tm, tpb=tpb, s_rows=s_rows, chunk=_pick(tm, 128, 8), eps=EPS)
    return pl.pallas_call(
        kern,
        grid=(r // tm, n // tn),
        in_specs=[
            pl.BlockSpec((tm, d), lambda i, j: (i, 0)),
            pl.BlockSpec((1, d), lambda i, j: (0, 0)),
            pl.BlockSpec((1, 2, d), lambda i, j: (i // tpb, 0, 0)),
            pl.BlockSpec((2, d), lambda i, j: (0, 0)),
            pl.BlockSpec((d, tn), lambda i, j: (0, j)),
        ],
        out_specs=pl.BlockSpec((tm, tn), lambda i, j: (i, j)),
        out_shape=jax.ShapeDtypeStruct((r, n), BF16),
        scratch_shapes=[pltpu.VMEM((tm, d), BF16)],
        compiler_params=_cparams(("parallel", "arbitrary")),
        name="proj_in",
    )(h, g, lat_mod, ctx_mod, w)


def _rope(r, cos, sin):
    return r * cos + pltpu.roll(r, V7X_LANES // 2, 1) * sin


def _mla_prep_kernel(cq_ref, ckv_ref, kr_ref, gqa_ref, gkva_ref, wq_ref, wk_ref, wv_ref, gq_ref, gk_ref,
                     cos_ref, sin_ref, q_ref, k_ref, vt_ref, *, heads, qk_dim, scale, eps):
    hd, hp = HEAD_DIM, V7X_MXU
    cqn = (_rms(cq_ref[...].astype(F32), eps) * gqa_ref[...]).astype(BF16)
    ckvn = (_rms(ckv_ref[...].astype(F32), eps) * gkva_ref[...]).astype(BF16)
    kr = kr_ref[...].astype(F32)
    cos, sin = cos_ref[...], sin_ref[...]
    kr2 = jnp.sum(kr * kr, axis=-1, keepdims=True)
    vt = _nt_dot(wv_ref[...], ckvn).astype(vt_ref.dtype)
    for c in range(vt_ref.shape[1]):
        vt_ref[0, c] = vt[:, c * VT_TILE:(c + 1) * VT_TILE]
    for h in range(heads):
        a = jnp.dot(cqn, wq_ref[:, h * hp:(h + 1) * hp], preferred_element_type=F32)
        inv = lax.rsqrt(jnp.sum(a * a, axis=-1, keepdims=True) / qk_dim + eps)
        an = a * inv * gq_ref[...]
        q_ref[:, h * hp:h * hp + hd] = (an[:, :hd] * scale).astype(q_ref.dtype)
        q_ref[:, h * hp + hd:(h + 1) * hp] = (_rope(an[:, hd:], cos, sin) * scale).astype(q_ref.dtype)
        kn = jnp.dot(ckvn, wk_ref[:, h * hd:(h + 1) * hd], preferred_element_type=F32)
        invk = lax.rsqrt((jnp.sum(kn * kn, axis=-1, keepdims=True) + kr2) / qk_dim + eps)
        k_ref[:, h * hp:h * hp + hd] = (kn * invk * gk_ref[:, :hd]).astype(k_ref.dtype)
        k_ref[:, h * hp + hd:(h + 1) * hp] = _rope(kr * invk * gk_ref[:, hd:], cos, sin).astype(k_ref.dtype)


def _mla_prep(px, lay, gqa, gkva, wq, wk, wv, gq, gk, cos, sin, *, t_rows, heads, qk_dim, scale):
    r = px.shape[0]
    tm = VT_TILE
    tpb = t_rows // tm
    vpt = tm // VT_TILE
    qr, kvr = gqa.shape[1], gkva.shape[1]
    kern = functools.partial(_mla_prep_kernel, heads=heads, qk_dim=qk_dim, scale=scale, eps=EPS)
    full = lambda a: pl.BlockSpec(a.shape, lambda i: (0,) * a.ndim, pipeline_mode=pl.Buffered(1))
    return pl.pallas_call(
        kern,
        grid=(r // tm,),
        in_specs=[
            pl.BlockSpec((tm, qr), lambda i: (i, lay["cq"] // qr)),
            pl.BlockSpec((tm, kvr), lambda i: (i, lay["ckv"] // kvr)),
            pl.BlockSpec((tm, V7X_LANES), lambda i: (i, lay["kr"] // V7X_LANES)),
            full(gqa), full(gkva), full(wq), full(wk), full(wv), full(gq), full(gk),
            pl.BlockSpec((tm, V7X_LANES), lambda i: (i % tpb, 0)),
            pl.BlockSpec((tm, V7X_LANES), lambda i: (i % tpb, 0)),
        ],
        out_specs=[
            pl.BlockSpec((tm, heads * V7X_MXU), lambda i: (i, 0)),
            pl.BlockSpec((tm, heads * V7X_MXU), lambda i: (i, 0)),
            pl.BlockSpec((1, vpt, heads * HEAD_DIM, VT_TILE), lambda i: (i // tpb, i % tpb, 0, 0)),
        ],
        out_shape=[
            jax.ShapeDtypeStruct((r, heads * V7X_MXU), BF16),
            jax.ShapeDtypeStruct((r, heads * V7X_MXU), BF16),
            jax.ShapeDtypeStruct((r // t_rows, t_rows // VT_TILE, heads * HEAD_DIM, VT_TILE), BF16),
        ],
        compiler_params=_cparams(("parallel",)),
        name="mla_prep",
    )(px, px, px, gqa, gkva, wq, wk, wv, gq, gk, cos, sin)


def _flash_kernel(q_ref, k_ref, vt_ref, o_ref, s0_ref, s1_ref, mx0_ref, mx1_ref, m_ref, l_ref, acc_ref, *, tk, nkv):
    m_ref[...] = jnp.full(m_ref.shape, -jnp.inf, F32)
    l_ref[...] = jnp.zeros(l_ref.shape, F32)
    acc_ref[...] = jnp.zeros(acc_ref.shape, F32)
    vpt = tk // VT_TILE

    def scores(j, s_ref, mx_ref):
        off = pl.multiple_of(j * tk, tk)
        st = _nt_dot(k_ref[0, pl.ds(off, tk), :], q_ref[0])
        s_ref[...] = st
        mx_ref[...] = jnp.max(st, axis=0, keepdims=True)

    def accumulate(j, s_ref, mx_ref):
        m_prev = m_ref[...]
        m_new = jnp.maximum(m_prev, mx_ref[...])
        alpha = jnp.exp2(m_prev - m_new)
        p = jnp.exp2(s_ref[...] - m_new)
        l_ref[...] = alpha * l_ref[...] + jnp.sum(p, axis=0, keepdims=True)
        pb = p.astype(BF16)
        pv = jnp.dot(vt_ref[0, j * vpt], pb[0:VT_TILE], preferred_element_type=F32)
        for c in range(1, vpt):
            pv = pv + jnp.dot(vt_ref[0, j * vpt + c], pb[c * VT_TILE:(c + 1) * VT_TILE],
                              preferred_element_type=F32)
        acc_ref[...] = alpha * acc_ref[...] + pv
        m_ref[...] = m_new

    scores(0, s0_ref, mx0_ref)
    npairs = (nkv - 1) // 2

    def pair(j):
        scores(j + 1, s1_ref, mx1_ref)
        accumulate(j, s0_ref, mx0_ref)
        scores(j + 2, s0_ref, mx0_ref)
        accumulate(j + 1, s1_ref, mx1_ref)

    def body(jj, carry):
        for u in range(FLASH_PAIRS_PER_TRIP):
            pair(2 * (jj * FLASH_PAIRS_PER_TRIP + u))
        return carry

    ntrips = npairs // FLASH_PAIRS_PER_TRIP
    lax.fori_loop(0, ntrips, body, 0)
    for pr in range(ntrips * FLASH_PAIRS_PER_TRIP, npairs):
        pair(2 * pr)
    if (nkv - 1) % 2 == 0:
        accumulate(nkv - 1, s0_ref, mx0_ref)
    else:
        scores(nkv - 1, s1_ref, mx1_ref)
        accumulate(nkv - 2, s0_ref, mx0_ref)
        accumulate(nkv - 1, s1_ref, mx1_ref)
    o_ref[0] = (acc_ref[...] / l_ref[...]).T.astype(o_ref.dtype)


def _flash(q, k, vt, *, heads, q_row0, n_q, kv_row0, n_kv, tq, tk, name):
    b = q.shape[0]
    dk, dv = V7X_MXU, HEAD_DIM
    assert q_row0 % tq == 0 and n_q % tq == 0 and kv_row0 % n_kv == 0 and n_kv % tk == 0
    assert tk % VT_TILE == 0
    kern = functools.partial(_flash_kernel, tk=tk, nkv=n_kv // tk)
    return pl.pallas_call(
        kern,
        grid=(b, heads, n_q // tq),
        in_specs=[
            pl.BlockSpec((1, tq, dk), lambda bi, h, i: (bi, q_row0 // tq + i, h)),
            pl.BlockSpec((1, n_kv, dk), lambda bi, h, i: (bi, kv_row0 // n_kv, h)),
            pl.BlockSpec((1, n_kv // VT_TILE, dv, VT_TILE), lambda bi, h, i: (bi, kv_row0 // n_kv, h, 0)),
        ],
        out_specs=pl.BlockSpec((1, tq, dv), lambda bi, h, i: (bi, i, h)),
        out_shape=jax.ShapeDtypeStruct((b, n_q, heads * dv), BF16),
        scratch_shapes=[pltpu.VMEM((tk, tq), F32), pltpu.VMEM((tk, tq), F32),
                        pltpu.VMEM((1, tq), F32), pltpu.VMEM((1, tq), F32),
                        pltpu.VMEM((1, tq), F32), pltpu.VMEM((1, tq), F32), pltpu.VMEM((dv, tq), F32)],
        compiler_params=_cparams(("parallel", "parallel", "arbitrary")),
        name=name,
    )(q, k, vt)


def _na_table_kernel(rpb_ref, tbl_ref, *, win_r, win_c, w):
    h = pl.program_id(0)
    q = lax.broadcasted_iota(jnp.int32, (w, 2 * w), 0)
    lane = lax.broadcasted_iota(jnp.int32, (w, 2 * w), 1)
    c0 = jnp.clip(q - win_c // 2, 0, w - win_c)
    n_d = 2 * win_r - 1
    for side in range(2):
        kc = lane - side * w
        inside = (kc >= c0) & (kc < c0 + win_c)
        own_half = (kc >= 0) & (kc < w)
        rel = kc - q + (win_c - 1)
        for d in range(n_d):
            acc = jnp.full((w, 2 * w), NEG, F32)
            for dc in range(2 * win_c - 1):
                acc = jnp.where(inside & (rel == dc), rpb_ref[h, d, dc] * LOG2E, acc)
            tbl_ref[0, side, d] = jnp.where(own_half, acc, 0.0)
        tbl_ref[0, side, n_d] = jnp.where(own_half, NEG, 0.0)


def _na_table(rpb):
    heads, n_d, n_c = rpb.shape
    win_r, win_c = (n_d + 1) // 2, (n_c + 1) // 2
    kern = functools.partial(_na_table_kernel, win_r=win_r, win_c=win_c, w=GRID_W)
    return pl.pallas_call(
        kern,
        grid=(heads,),
        in_specs=[pl.BlockSpec(memory_space=pltpu.SMEM)],
        out_specs=pl.BlockSpec((1, 2, n_d + 1, GRID_W, 2 * GRID_W), lambda h: (h, 0, 0, 0, 0)),
        out_shape=jax.ShapeDtypeStruct((heads, 2, n_d + 1, GRID_W, 2 * GRID_W), F32),
        compiler_params=_cparams(("arbitrary",)),
        name="na_table",
    )(rpb)


def _na_kernel(q_ref, k_ref, v_ref, gq_ref, gk_ref, tbl_ref, o_ref, kn_ref, bias_ref, sl0_ref, sc0_ref,
               sl1_ref, sc1_ref, *, s_rows, l_rows, grid_rows, win_r, scale, eps, chunk):
    w = GRID_W
    t_rows = s_rows + l_rows
    nq, nk = NA_QROWS * w, NA_KROWS * w

    def kn_body(c, carry):
        r0 = pl.multiple_of(c * chunk, chunk)
        kk = k_ref[0, pl.ds(r0, chunk), :].astype(F32)
        kn_ref[pl.ds(r0, chunk), :] = (_rms(kk, eps) * gk_ref[...]).astype(BF16)
        return carry

    lax.fori_loop(0, t_rows // chunk, kn_body, 0)
    kc = kn_ref[s_rows:t_rows, :]
    vc = v_ref[0, s_rows:t_rows, :]

    def norm_q(x):
        return (_rms(x.astype(F32), eps) * gq_ref[...] * scale).astype(BF16)

    def softmax_pv(parts):
        m = functools.reduce(jnp.maximum, [jnp.max(s, axis=-1, keepdims=True) for s, _ in parts])
        ps = [jnp.exp2(s - m) for s, _ in parts]
        den = functools.reduce(lambda a, b: a + b, [jnp.sum(p, axis=-1, keepdims=True) for p in ps])
        num = functools.reduce(lambda a, b: a + b,
                               [jnp.dot(p.astype(BF16), vv, preferred_element_type=F32)
                                for p, (_, vv) in zip(ps, parts)])
        return num / den

    def key_start(i):
        return jnp.clip(i * NA_QROWS - win_r // 2, 0, grid_rows - NA_KROWS)

    def scores(i, sl_ref, sc_ref):
        q0 = pl.multiple_of(i * nq, nq)
        ks = key_start(i)
        k0 = pl.multiple_of(ks * w, w)
        qn = norm_q(q_ref[0, pl.ds(q0, nq), :])
        for qr in range(NA_QROWS):
            r = i * NA_QROWS + qr
            r0 = jnp.clip(r - win_r // 2, 0, grid_rows - win_r)
            for p in range(NA_KROWS // 2):
                ds = []
                for side in range(2):
                    kr = ks + 2 * p + side
                    ok = (kr >= r0) & (kr < r0 + win_r)
                    ds.append(jnp.where(ok, kr - r + (win_r - 1), 2 * win_r - 1))
                bias_ref[qr * w:(qr + 1) * w, p * 2 * w:(p + 1) * 2 * w] = (
                    tbl_ref[0, 0, ds[0]] + tbl_ref[0, 1, ds[1]])
        sl_ref[...] = _nt_dot(qn, kn_ref[pl.ds(k0, nk), :]) + bias_ref[...]
        sc_ref[...] = _nt_dot(qn, kc)

    def attend(i, sl_ref, sc_ref):
        q0 = pl.multiple_of(i * nq, nq)
        k0 = pl.multiple_of(key_start(i) * w, w)
        o = softmax_pv([(sc_ref[...], vc), (sl_ref[...], v_ref[0, pl.ds(k0, nk), :])])
        o_ref[0, pl.ds(q0, nq), :] = o.astype(o_ref.dtype)

    nblk = grid_rows // NA_QROWS
    scores(0, sl0_ref, sc0_ref)
    npairs = (nblk - 1) // 2

    def pair_body(jj, carry):
        i = 2 * jj
        scores(i + 1, sl1_ref, sc1_ref)
        attend(i, sl0_ref, sc0_ref)
        scores(i + 2, sl0_ref, sc0_ref)
        attend(i + 1, sl1_ref, sc1_ref)
        return carry

    lax.fori_loop(0, npairs, pair_body, 0)
    if (nblk - 1) % 2 == 0:
        attend(nblk - 1, sl0_ref, sc0_ref)
    else:
        scores(nblk - 1, sl1_ref, sc1_ref)
        attend(nblk - 2, sl0_ref, sc0_ref)
        attend(nblk - 1, sl1_ref, sc1_ref)

    qc = norm_q(q_ref[0, s_rows:t_rows, :])
    o_ref[0, s_rows:t_rows, :] = softmax_pv([(_nt_dot(qc, kc), vc)]).astype(o_ref.dtype)


def _na_attention(px3, lay, gq, gk, tbl, *, heads, s_rows, l_rows, win_r, scale):
    b, t, _ = px3.shape
    hd = HEAD_DIM
    grid_rows = s_rows // GRID_W
    assert 2 * GRID_W == V7X_LANES and grid_rows % NA_QROWS == 0 and grid_rows >= NA_KROWS
    assert NA_KROWS >= NA_QROWS + win_r - 1 and NA_KROWS % 2 == 0
    kern = functools.partial(_na_kernel, s_rows=s_rows, l_rows=l_rows, grid_rows=grid_rows, win_r=win_r,
                             scale=scale, eps=EPS, chunk=_pick(t, 256, 16))
    col = lambda name: (lambda bi, h: (bi, 0, lay[name] // hd + h))
    return pl.pallas_call(
        kern,
        grid=(b, heads),
        in_specs=[
            pl.BlockSpec((1, t, hd), col("nq")),
            pl.BlockSpec((1, t, hd), col("nk")),
            pl.BlockSpec((1, t, hd), col("nv")),
            pl.BlockSpec((1, hd), lambda bi, h: (0, 0)),
            pl.BlockSpec((1, hd), lambda bi, h: (0, 0)),
            pl.BlockSpec((1,) + tbl.shape[1:], lambda bi, h: (h, 0, 0, 0, 0)),
        ],
        out_specs=pl.BlockSpec((1, t, hd), lambda bi, h: (bi, 0, h)),
        out_shape=jax.ShapeDtypeStruct((b, t, heads * hd), BF16),
        scratch_shapes=[pltpu.VMEM((t, hd), BF16), pltpu.VMEM((NA_QROWS * GRID_W, NA_KROWS * GRID_W), F32),
                        pltpu.VMEM((NA_QROWS * GRID_W, NA_KROWS * GRID_W), F32),
                        pltpu.VMEM((NA_QROWS * GRID_W, l_rows), F32),
                        pltpu.VMEM((NA_QROWS * GRID_W, NA_KROWS * GRID_W), F32),
                        pltpu.VMEM((NA_QROWS * GRID_W, l_rows), F32)],
        compiler_params=_cparams(("parallel", "parallel")),
        name="na_attn",
    )(px3, px3, px3, gq, gk, tbl)


def _conv_kernel(cb_ref, cc_ref, cx_ref, ccp_ref, cxp_ref, ccn_ref, cxn_ref, w_ref, o_ref, *,
                 tm, tpb, s_rows, t_rows, halo):
    u = cc_ref[...].astype(F32) * cx_ref[...].astype(F32)
    u_before = ccp_ref[halo - 1:halo, :].astype(F32) * cxp_ref[halo - 1:halo, :].astype(F32)
    u_after = ccn_ref[0:1, :].astype(F32) * cxn_ref[0:1, :].astype(F32)
    loc = lax.broadcasted_iota(jnp.int32, (tm, 1), 0)
    row = (pl.program_id(0) % tpb) * tm + loc
    first = (row == 0) | (row == s_rows)
    last = (row == s_rows - 1) | (row == t_rows - 1)
    u_prev = jnp.where(loc == 0, u_before, pltpu.roll(u, 1, 0))
    u_prev = jnp.where(first, 0.0, u_prev)
    u_next = jnp.where(loc == tm - 1, u_after, pltpu.roll(u, tm - 1, 0))
    u_next = jnp.where(last, 0.0, u_next)
    y = w_ref[0:1, :] * u_prev + w_ref[1:2, :] * u + w_ref[2:3, :] * u_next
    o_ref[...] = (cb_ref[...].astype(F32) * y).astype(o_ref.dtype)


def _gated_conv(px, lay, w, *, t_rows, s_rows):
    r = px.shape[0]
    c = w.shape[1]
    halo = 16
    tm = _pick(t_rows, 256, 128)
    tpb = t_rows // tm
    nb = r // halo
    kern = functools.partial(_conv_kernel, tm=tm, tpb=tpb, s_rows=s_rows, t_rows=t_rows, halo=halo)
    blk = lambda name: pl.BlockSpec((tm, c), lambda i: (i, lay[name] // c))
    prev = lambda name: pl.BlockSpec((halo, c), lambda i: (jnp.maximum(i * (tm // halo) - 1, 0), lay[name] // c))
    nxt = lambda name: pl.BlockSpec((halo, c), lambda i: (jnp.minimum((i + 1) * (tm // halo), nb - 1), lay[name] // c))
    return pl.pallas_call(
        kern,
        grid=(r // tm,),
        in_specs=[blk("cb"), blk("cc"), blk("cx"), prev("cc"), prev("cx"), nxt("cc"), nxt("cx"),
                  pl.BlockSpec(w.shape, lambda i: (0, 0))],
        out_specs=pl.BlockSpec((tm, c), lambda i: (i, 0)),
        out_shape=jax.ShapeDtypeStruct((r, c), BF16),
        compiler_params=_cparams(("parallel",)),
        name="gated_conv",
    )(px, px, px, px, px, px, px, w)


def _proj_out_kernel(oa_ref, ob_ref, oc_ref, wa_ref, wb_ref, wc_ref, h_ref, lat_ref, ctx_ref, o_ref, *,
                     tm, tpb, s_rows):
    y = jnp.dot(oa_ref[...], wa_ref[...], preferred_element_type=F32)
    y = y + jnp.dot(ob_ref[...], wb_ref[...], preferred_element_type=F32)
    y = y + jnp.dot(oc_ref[...], wc_ref[...], preferred_element_type=F32)
    is_ctx = _row_is_ctx(pl.program_id(0) % tpb, tm, 0, tm, s_rows)
    gate = jnp.where(is_ctx, ctx_ref[...], lat_ref[0])
    o_ref[...] = h_ref[...] + gate * y


def _proj_out(oa, ob, oc, wa, wb, wc, h, lat_gate, ctx_gate, *, t_rows, s_rows):
    r, d = h.shape
    tm = _pick(t_rows, 768, 128)
    tn = _pick(d, 1024, V7X_LANES)
    tpb = t_rows // tm
    kern = functools.partial(_proj_out_kernel, tm=tm, tpb=tpb, s_rows=s_rows)
    lhs = lambda a: pl.BlockSpec((tm, a.shape[1]), lambda i, j: (i, 0))
    rhs = lambda a: pl.BlockSpec((a.shape[0], tn), lambda i, j: (0, j))
    return pl.pallas_call(
        kern,
        grid=(r // tm, d // tn),
        in_specs=[lhs(oa), lhs(ob), lhs(oc), rhs(wa), rhs(wb), rhs(wc),
                  pl.BlockSpec((tm, tn), lambda i, j: (i, j)),
                  pl.BlockSpec((1, 1, tn), lambda i, j: (i // tpb, 0, j)),
                  pl.BlockSpec((1, tn), lambda i, j: (0, j))],
        out_specs=pl.BlockSpec((tm, tn), lambda i, j: (i, j)),
        out_shape=jax.ShapeDtypeStruct((r, d), F32),
        input_output_aliases={6: 0},
        compiler_params=_cparams(("parallel", "arbitrary")),
        name="proj_out",
    )(oa, ob, oc, wa, wb, wc, h, lat_gate, ctx_gate)


def _moe_norm_kernel(h_ref, g_ref, lat_ref, ctx_ref, wr_ref, hn_ref, aff_ref, *, tm, tpb, s_rows, n_exp, eps):
    is_ctx = _row_is_ctx(pl.program_id(0) % tpb, tm, 0, tm, s_rows)
    hn = _modulated_norm(h_ref[...], g_ref[...], lat_ref[0, 0:1, :], lat_ref[0, 1:2, :],
                         ctx_ref[0:1, :], ctx_ref[1:2, :], is_ctx, eps)
    hb = hn.astype(BF16)
    half = hn.shape[1] // 2
    hf = hb.astype(F32)
    hn_ref[...] = pltpu.bitcast(hf[:, :half], jnp.uint32) | (pltpu.bitcast(hf[:, half:], jnp.uint32) >> 16)
    logits = jnp.dot(hb, wr_ref[...], preferred_element_type=F32)
    lane = lax.broadcasted_iota(jnp.int32, logits.shape, 1)
    logits = jnp.where(lane < n_exp, logits, -jnp.inf)
    e = jnp.exp(logits - jnp.max(logits, axis=-1, keepdims=True))
    aff = e / jnp.sum(e, axis=-1, keepdims=True)
    aff_ref[0] = aff.T[:n_exp, :]


def _moe_norm(h, g, lat_mod, ctx_mod, wr, *, b, t_rows, s_rows, n_exp):
    r, d = h.shape
    tm = _pick(t_rows, 256, 128)
    tpb = t_rows // tm
    kern = functools.partial(_moe_norm_kernel, tm=tm, tpb=tpb, s_rows=s_rows, n_exp=n_exp, eps=EPS)
    return pl.pallas_call(
        kern,
        grid=(r // tm,),
        in_specs=[
            pl.BlockSpec((tm, d), lambda i: (i, 0)),
            pl.BlockSpec((1, d), lambda i: (0, 0)),
            pl.BlockSpec((1, 2, d), lambda i: (i // tpb, 0, 0)),
            pl.BlockSpec((2, d), lambda i: (0, 0)),
            pl.BlockSpec(wr.shape, lambda i: (0, 0)),
        ],
        out_specs=[pl.BlockSpec((tm, d // 2), lambda i: (i, 0)),
                   pl.BlockSpec((1, n_exp, tm), lambda i: (i // tpb, 0, i % tpb))],
        out_shape=[jax.ShapeDtypeStruct((r, d // 2), jnp.uint32), jax.ShapeDtypeStruct((b, n_exp, t_rows), F32)],
        compiler_params=_cparams(("parallel",)),
        name="moe_norm",
    )(h, g, lat_mod, ctx_mod, wr)


def _topk_kernel(aff_ref, tri_ref, ones_ref, idx_ref, gate_ref, posm_ref, *, n, k, n_exp):
    ln = V7X_LANES
    nt = n // ln
    a = aff_ref[0]
    bits = pltpu.bitcast(a, jnp.int32)
    thr = jnp.zeros((n_exp, 1), jnp.int32)
    for bit in range(30, -1, -1):
        cand = thr | (1 << bit)
        cnt = jnp.sum((bits >= cand).astype(jnp.int32), axis=1, keepdims=True)
        thr = jnp.where(cnt >= k, cand, thr)
    gt = bits > thr
    eq = bits == thr

    def excl_prefix(x):
        xb = x.astype(BF16)
        run = jnp.zeros((n_exp, ln), F32)
        outs = []
        for t in range(nt):
            blk = xb[:, t * ln:(t + 1) * ln]
            outs.append(jnp.dot(blk, tri_ref[...], preferred_element_type=F32) + run)
            run = run + jnp.dot(blk, ones_ref[...], preferred_element_type=F32)
        return jnp.concatenate(outs, axis=1)

    need = (k - jnp.sum(gt.astype(jnp.int32), axis=1, keepdims=True)).astype(F32)
    eqf = jnp.where(eq, 1.0, 0.0)
    sel = gt | (eq & (excl_prefix(eqf) < need))
    pos = excl_prefix(jnp.where(sel, 1.0, 0.0))
    posm_ref[...] = jnp.where(sel, pos, -1.0)

    lane = lax.broadcasted_iota(jnp.int32, (8, ln), 1)
    lane_f = lane.astype(F32)

    def jb_body(jb, carry):
        slot = (jb * 8 + lax.broadcasted_iota(jnp.int32, (8, 1), 0)).astype(F32)
        out_i = jnp.zeros((8, ln), F32)
        out_g = jnp.zeros((8, ln), F32)
        for e in range(n_exp):
            acc_i = jnp.zeros((8, ln), F32)
            acc_g = jnp.zeros((8, ln), F32)
            for t in range(nt):
                hit = posm_ref[e:e + 1, t * ln:(t + 1) * ln] == slot
                acc_i = acc_i + jnp.where(hit, lane_f + float(t * ln), 0.0)
                acc_g = acc_g + jnp.where(hit, aff_ref[0, e:e + 1, t * ln:(t + 1) * ln], 0.0)
            out_i = jnp.where(lane == e, jnp.sum(acc_i, axis=1, keepdims=True), out_i)
            out_g = jnp.where(lane == e, jnp.sum(acc_g, axis=1, keepdims=True), out_g)
        r0 = pl.multiple_of(jb * 8, 8)
        idx_ref[0, pl.ds(r0, 8), :] = out_i.astype(jnp.int32)
        gate_ref[0, pl.ds(r0, 8), :] = out_g
        return carry

    lax.fori_loop(0, k // 8, jb_body, 0)


def _topk(aff, tri, ones, *, row0, n, k):
    b, n_exp, _ = aff.shape
    assert row0 % n == 0 and n % V7X_LANES == 0 and k % 8 == 0 and n_exp <= V7X_LANES
    kern = functools.partial(_topk_kernel, n=n, k=k, n_exp=n_exp)
    return pl.pallas_call(
        kern,
        grid=(b,),
        in_specs=[pl.BlockSpec((1, n_exp, n), lambda bi: (bi, 0, row0 // n)),
                  pl.BlockSpec(tri.shape, lambda bi: (0, 0)),
                  pl.BlockSpec(ones.shape, lambda bi: (0, 0))],
        out_specs=[pl.BlockSpec((1, k, V7X_LANES), lambda bi: (bi, 0, 0)),
                   pl.BlockSpec((1, k, V7X_LANES), lambda bi: (bi, 0, 0))],
        out_shape=[jax.ShapeDtypeStruct((b, k, V7X_LANES), jnp.int32),
                   jax.ShapeDtypeStruct((b, k, V7X_LANES), F32)],
        scratch_shapes=[pltpu.VMEM((n_exp, n), F32)],
        compiler_params=_cparams(("parallel",)),
        name="topk",
    )(aff, tri, ones)


ROW_UNROLL = 16
SEM_HN, SEM_H, SEM_OUT = 0, 2, 3


def _expert_kernel(idx_ref, g_ref, lat_ref, ctx_ref, wg_ref, wu_ref, wd_ref, hn_hbm, h_hbm, out_hbm,
                   xe_ref, y_ref, st_ref, sem, *, rows, cap_lat, dcols, nchunk):
    del h_hbm
    e, bi, f = pl.program_id(0), pl.program_id(1), pl.program_id(2)
    ne, nb, nf = pl.num_programs(0), pl.num_programs(1), pl.num_programs(2)
    ch = rows // nchunk
    half = xe_ref.shape[1] // 2

    def issue_rows(n_rows, make, both_priorities=False):
        def trip(jj, carry):
            base = pl.multiple_of(jj * ROW_UNROLL, ROW_UNROLL)
            for u in range(ROW_UNROLL):
                make(base, u).start(priority=u % 2 if both_priorities else 0)
            return carry

        lax.fori_loop(0, n_rows // ROW_UNROLL, trip, 0)

    def hn_row(c, base, u):
        return pltpu.make_async_copy(hn_hbm.at[pl.ds(idx_ref[0, 0, 0, c * ch + base + u], 1)],
                                     st_ref.at[c % 2, pl.ds(base + u, 1)], sem.at[SEM_HN + c % 2])

    def hn_chunk_all(c):
        return pltpu.make_async_copy(hn_hbm.at[pl.ds(0, ch)], st_ref.at[c % 2], sem.at[SEM_HN + c % 2])

    def h_row(base, u):
        return pltpu.make_async_copy(out_hbm.at[pl.ds(idx_ref[0, 0, 0, base + u], 1)],
                                     y_ref.at[pl.ds(base + u, 1)], sem.at[SEM_H])

    def out_row(base, u):
        return pltpu.make_async_copy(y_ref.at[pl.ds(base + u, 1)],
                                     out_hbm.at[pl.ds(idx_ref[0, 0, 0, base + u], 1)], sem.at[SEM_OUT])

    def all_rows(s):
        return pltpu.make_async_copy(out_hbm.at[pl.ds(0, rows)], y_ref, sem.at[s])

    @pl.when(f == 0)
    def _():
        issue_rows(ch, functools.partial(hn_row, 0))

        @pl.when((e > 0) | (bi > 0))
        def _():
            all_rows(SEM_OUT).wait()

        issue_rows(rows, h_row)
        for c in range(nchunk):
            if c + 1 < nchunk:
                issue_rows(ch, functools.partial(hn_row, c + 1))
            hn_chunk_all(c).wait()
            p = st_ref[c % 2]
            hi = pltpu.bitcast(p & jnp.uint32(0xFFFF0000), F32)
            lo = pltpu.bitcast(p << 16, F32)
            xe_ref[c * ch:(c + 1) * ch, :half] = hi.astype(BF16)
            xe_ref[c * ch:(c + 1) * ch, half:] = lo.astype(BF16)

    xe = xe_ref[...]
    a = jnp.dot(xe, wg_ref[0, 0].astype(BF16), preferred_element_type=F32)
    u = jnp.dot(xe, wu_ref[0, 0].astype(BF16), preferred_element_type=F32)
    hm = ((a / (1.0 + jnp.exp(-a))) * u).astype(BF16)

    @pl.when(f == 0)
    def _():
        all_rows(SEM_H).wait()

    g = g_ref[0, 0]
    d = y_ref.shape[1]
    for c in range(d // dcols):
        cs = slice(c * dcols, (c + 1) * dcols)
        y = jnp.dot(hm, wd_ref[0, 0, :, cs].astype(BF16), preferred_element_type=F32) * g
        y_ref[0:cap_lat, cs] = y_ref[0:cap_lat, cs] + lat_ref[0, :, cs] * y[0:cap_lat]
        y_ref[cap_lat:rows, cs] = y_ref[cap_lat:rows, cs] + ctx_ref[:, cs] * y[cap_lat:rows]

    @pl.when(f == nf - 1)
    def _():
        issue_rows(rows, out_row, both_priorities=True)

        @pl.when((e == ne - 1) & (bi == nb - 1))
        def _():
            all_rows(SEM_OUT).wait()


def _experts(idx, gates, lat_gate, ctx_gate, wg, wu, wd, layer, hn, h, *, cap_lat):
    n_exp, b, _, rows = idx.shape
    d, ff = wg.shape[2], wg.shape[3]
    fc = _pick(ff, 256, V7X_LANES)
    nchunk = max(n for n in range(1, 7) if rows % (16 * n) == 0)
    assert rows % 16 == 0 and hn.shape[1] * 2 == d
    kern = functools.partial(_expert_kernel, rows=rows, cap_lat=cap_lat, dcols=_pick(d, 1024, V7X_LANES),
                             nchunk=nchunk)
    assert (rows // nchunk) % ROW_UNROLL == 0
    return pl.pallas_call(
        kern,
        grid=(n_exp, b, ff // fc),
        in_specs=[
            pl.BlockSpec((1, 1, 1, rows), lambda e, bi, f: (e, bi, 0, 0), memory_space=pltpu.SMEM),
            pl.BlockSpec((1, 1, rows, 1), lambda e, bi, f: (e, bi, 0, 0)),
            pl.BlockSpec((1, 1, d), lambda e, bi, f: (bi, 0, 0)),
            pl.BlockSpec((1, d), lambda e, bi, f: (0, 0)),
            pl.BlockSpec((1, 1, d, fc), lambda e, bi, f: (layer, e, 0, f)),
            pl.BlockSpec((1, 1, d, fc), lambda e, bi, f: (layer, e, 0, f)),
            pl.BlockSpec((1, 1, fc, d), lambda e, bi, f: (layer, e, f, 0)),
            pl.BlockSpec(memory_space=pl.ANY),
            pl.BlockSpec(memory_space=pl.ANY),
        ],
        out_specs=pl.BlockSpec(memory_space=pl.ANY),
        out_shape=jax.ShapeDtypeStruct(h.shape, F32),
        scratch_shapes=[pltpu.VMEM((rows, d), BF16), pltpu.VMEM((rows, d), F32),
                        pltpu.VMEM((2, rows // nchunk, d // 2), jnp.uint32), pltpu.SemaphoreType.DMA((4,))],
        input_output_aliases={8: 0},
        compiler_params=_cparams(("arbitrary", "arbitrary", "arbitrary")),
        name="experts",
    )(idx, gates, lat_gate, ctx_gate, wg, wu, wd, hn, h)


def _in_layout(conv_ch, q_rank, kv_rank, na_cols):
    lay, off = {}, 0

    def put(name, width, align):
        nonlocal off
        off = -(-off // align) * align
        lay[name] = off
        off += width

    for name in ("cb", "cc", "cx"):
        put(name, conv_ch, conv_ch)
    put("cq", q_rank, q_rank)
    for name in ("nq", "nk", "nv"):
        put(name, na_cols, HEAD_DIM)
    put("ckv", kv_rank, kv_rank)
    put("kr", V7X_LANES, V7X_LANES)
    lay["n"] = -(-off // 512) * 512
    return lay


def _spread_rope(a):
    half = MLA_ROPE // 2
    z = jnp.zeros(a.shape[:-1] + (V7X_LANES // 2 - half,), a.dtype)
    return jnp.concatenate([a[..., :half], z, a[..., half:], z], axis=-1)


def _pad_heads(a, heads, nope):
    a = a.reshape(a.shape[:-1] + (heads, nope + MLA_ROPE))
    a = jnp.concatenate([a[..., :nope], _spread_rope(a[..., nope:])], axis=-1)
    return a.reshape(a.shape[:-2] + (heads * V7X_MXU,))


def _rope_tables(s_rows, l_rows):
    pos = jnp.arange(s_rows, dtype=jnp.int32)
    row = (pos // GRID_W).astype(F32)
    col = (pos % GRID_W).astype(F32)
    inv = jnp.power(ROPE_BASE, -jnp.arange(ROPE_F, dtype=F32) / ROPE_F)
    ang = jnp.concatenate([row[:, None] * inv, col[:, None] * inv], axis=1)
    cos = jnp.concatenate([jnp.cos(ang), jnp.ones((l_rows, 2 * ROPE_F), F32)], axis=0)
    sin = jnp.concatenate([jnp.sin(ang), jnp.zeros((l_rows, 2 * ROPE_F), F32)], axis=0)
    return _spread_rope(jnp.concatenate([cos, cos], axis=1)), _spread_rope(jnp.concatenate([-sin, sin], axis=1))


def kernel(x, c, ctx, c_ctx, ada_down, ada_up, ada_bias, norm1_g, w_in, mla_qa_g, mla_kva_g, mla_w_uq, mla_w_ukv, mla_q_g, mla_k_g, conv_w, na_q_g, na_k_g, na_rpb, w_out, norm2_g, w_router, ex_gate, ex_up, ex_down):
    b, s_rows, d = x.shape
    l_rows = ctx.shape[1]
    t_rows = s_rows + l_rows
    depth = w_in.shape[0]
    q_rank, kv_rank = mla_qa_g.shape[1], mla_kva_g.shape[1]
    qk_dim = mla_q_g.shape[1]
    nope = qk_dim - MLA_ROPE
    assert nope == HEAD_DIM
    mla_heads = mla_w_uq.shape[2] // qk_dim
    conv_ch = conv_w.shape[2]
    na_heads = na_rpb.shape[1]
    na_cols = na_heads * HEAD_DIM
    win_r = (na_rpb.shape[2] + 1) // 2
    n_exp = w_router.shape[2]
    cap_lat = CAP_FACTOR * s_rows // n_exp
    cap_ctx = CAP_FACTOR * l_rows // n_exp
    lay = _in_layout(conv_ch, q_rank, kv_rank, na_cols)
    mla_cols = mla_heads * HEAD_DIM

    o_cq = 0
    o_ckv = o_cq + q_rank
    o_kr = o_ckv + kv_rank
    o_cb = o_kr + MLA_ROPE
    o_cc = o_cb + conv_ch
    o_cx = o_cc + conv_ch
    o_nq = o_cx + conv_ch
    o_nk = o_nq + na_cols
    o_nv = o_nk + na_cols

    n_cond = -(-(b + 1) // 8) * 8
    cond = jnp.zeros((n_cond, d), F32).at[:b].set(c).at[b].set(c_ctx)
    mods = _ada_mod(cond, ada_down, ada_up, ada_bias).reshape(depth, n_cond, N_MOD, d)

    cos_t, sin_t = _rope_tables(s_rows, l_rows)
    tri = jnp.triu(jnp.ones((V7X_LANES, V7X_LANES), BF16), 1)
    ones = jnp.ones((V7X_LANES, V7X_LANES), BF16)
    tok0 = (jnp.arange(b, dtype=jnp.int32) * t_rows)[None, :, None]

    h = jnp.concatenate([x, ctx], axis=1).reshape(b * t_rows, d)

    for l in range(depth):
        m = mods[l]
        lat = lambda i, j: jnp.stack([m[:b, i], m[:b, j]], axis=1)
        cxm = lambda i, j: jnp.stack([m[b, i], m[b, j]], axis=0)

        wl = w_in[l]
        sec ={"cb": wl[:, o_cb:o_cc], "cc": wl[:, o_cc:o_cx], "cx": wl[:, o_cx:o_nq], "cq": wl[:, o_cq:o_ckv],
               "nq": wl[:, o_nq:o_nk], "nk": wl[:, o_nk:o_nv], "nv": wl[:, o_nv:o_nv + na_cols],
               "ckv": wl[:, o_ckv:o_kr], "kr": _spread_rope(wl[:, o_kr:o_cb])}
        w_p = jnp.zeros((d, lay["n"]), BF16)
        for name, blk in sec.items():
            w_p = lax.dynamic_update_slice(w_p, blk.astype(BF16), (0, lay[name]))
        wq = _pad_heads(mla_w_uq[l], mla_heads, nope).astype(BF16)
        wkv = mla_w_ukv[l].reshape(kv_rank, mla_heads, 2 * HEAD_DIM)
        wk = wkv[:, :, :HEAD_DIM].reshape(kv_rank, mla_cols).astype(BF16)
        wv = wkv[:, :, HEAD_DIM:].reshape(kv_rank, mla_cols).T.astype(BF16)
        gq = _pad_heads(mla_q_g[l][None, :], 1, nope)
        gk = _pad_heads(mla_k_g[l][None, :], 1, nope)
        wo = w_out[l].astype(BF16)
        wo_a, wo_b, wo_c = wo[:mla_cols], wo[mla_cols:mla_cols + conv_ch], wo[mla_cols + conv_ch:]
        wr = jnp.zeros((d, V7X_LANES), BF16).at[:, :n_exp].set(w_router[l].astype(BF16))

        px = _proj_in(h, norm1_g[l][None, :], lat(0, 1), cxm(0, 1), w_p, t_rows=t_rows, s_rows=s_rows)
        q, k, vt = _mla_prep(px, lay, mla_qa_g[l][None, :], mla_kva_g[l][None, :], wq, wk, wv, gq, gk, cos_t, sin_t,
                            t_rows=t_rows, heads=mla_heads, qk_dim=qk_dim, scale=qk_dim ** -0.5 * LOG2E)
        q3, k3 = q.reshape(b, t_rows, -1), k.reshape(b, t_rows, -1)
        o_a = jnp.concatenate([
            _flash(q3, k3, vt, heads=mla_heads, q_row0=0, n_q=s_rows, kv_row0=0, n_kv=t_rows,
                   tq=_pick(s_rows, 2048, 256), tk=_pick(t_rows, 768, VT_TILE), name="mla_attn"),
            _flash(q3, k3, vt, heads=mla_heads, q_row0=s_rows, n_q=l_rows, kv_row0=s_rows, n_kv=l_rows,
                   tq=l_rows, tk=l_rows, name="mla_attn_ctx")], axis=1)
        o_b = _gated_conv(px, lay, conv_w[l], t_rows=t_rows, s_rows=s_rows)
        o_c = _na_attention(px.reshape(b, t_rows, -1), lay, na_q_g[l][None, :], na_k_g[l][None, :],
                            _na_table(na_rpb[l]), heads=na_heads, s_rows=s_rows, l_rows=l_rows, win_r=win_r,
                            scale=HEAD_DIM ** -0.5 * LOG2E)
        h = _proj_out(o_a.reshape(b * t_rows, -1), o_b, o_c.reshape(b * t_rows, -1), wo_a, wo_b, wo_c, h,
                      m[:b, 2][:, None, :], m[b, 2][None, :], t_rows=t_rows, s_rows=s_rows)

        hn, aff = _moe_norm(h, norm2_g[l][None, :], lat(3, 4), cxm(3, 4), wr, b=b, t_rows=t_rows, s_rows=s_rows,
                            n_exp=n_exp)
        i_lat, g_lat = _topk(aff, tri, ones, row0=0, n=s_rows, k=cap_lat)
        i_ctx, g_ctx = _topk(aff, tri, ones, row0=s_rows, n=l_rows, k=cap_ctx)
        em = lambda a: jnp.transpose(a[:, :, :n_exp], (2, 0, 1))
        idx = jnp.concatenate([em(i_lat) + tok0, em(i_ctx) + tok0 + s_rows], axis=2)
        gates = jnp.concatenate([em(g_lat), em(g_ctx)], axis=2)
        h = _experts(idx[:, :, None, :], gates[..., None], m[:b, 5][:, None, :], m[b, 5][None, :],
                     ex_gate, ex_up, ex_down, l, hn, h, cap_lat=cap_lat)

    return h.reshape(b, t_rows, d)[:, :s_rows]
```
